```python
import math, functools
import jax, jax.numpy as jnp
from jax import lax
import numpy as np

D_MODEL = 2048
BATCH = 16
SEQ = 2048
DEPTH = 1
DEC_BATCH = 32
DEC_SEQ = 4
PAST_LEN = 16384
PAGE_SIZE = 128

HEAD_DIM_A = 128
N_HEADS_A = D_MODEL // HEAD_DIM_A
D_A = N_HEADS_A * HEAD_DIM_A
MOBA_BLOCK = 256
MOBA_TOP_K = 3
Q_BLOCK = 128
N_HEADS_B = 4
DK_B = D_MODEL // (2 * N_HEADS_B)
DV_B = D_MODEL // N_HEADS_B
D_QK_B = N_HEADS_B * DK_B
D_B = N_HEADS_B * DV_B
MLSTM_CHUNK = 64
FORGET_BIAS = 3.0
D_FF = ((8 * D_MODEL // 3 + 127) // 128) * 128
SPLIT_SIZES = (D_A, D_A, D_A, D_QK_B, D_QK_B, D_B, D_B, 2 * N_HEADS_B, D_MODEL, D_MODEL)
SPLIT_POINTS = tuple(int(s) for s in np.cumsum(SPLIT_SIZES)[:-1])
N_IN = int(sum(SPLIT_SIZES))
EPS = 1e-6
NEG_INF = -1e30

kernel_name = 'moba_mlstm_macaron_decode_step'


def rms_norm(x, g):
    xf = x.astype(jnp.float32)
    y = xf * lax.rsqrt(jnp.mean(xf * xf, axis=-1, keepdims=True) + EPS)
    return (y * g.astype(jnp.float32)).astype(x.dtype)


def swiglu(x, w_gate, w_up, w_down):
    return (jax.nn.silu(x @ w_gate) * (x @ w_up)) @ w_down


def moba_prompt(q, k, v):
    B, S, H, Dh = q.shape
    f32 = jnp.float32
    scale = Dh ** -0.5
    nb = -(-S // MOBA_BLOCK)
    pad = nb * MOBA_BLOCK - S
    def blocks(a):
        a = jnp.pad(a, ((0, 0), (0, pad), (0, 0), (0, 0)))
        return a.reshape(B, nb, MOBA_BLOCK, H, Dh).transpose(0, 3, 1, 2, 4)
    kb, vb = blocks(k), blocks(v)
    k_mean = jnp.mean(kb.astype(f32), axis=3)
    n_sel = min(MOBA_TOP_K, nb - 1)
    nq = S // Q_BLOCK
    q_blocks = q.reshape(B, nq, Q_BLOCK, H, Dh).reshape(B * nq, Q_BLOCK, H, Dh)
    b_ids = jnp.repeat(jnp.arange(B), nq)
    c_ids = jnp.tile(jnp.arange(nq), B)
    h_ix = jnp.arange(H)[None, :, None]

    def one_block(args):
        b, c, qc = args
        kb_b, vb_b = kb[b], vb[b]
        start = c * Q_BLOCK
        t = start + jnp.arange(Q_BLOCK)
        cb = start // MOBA_BLOCK
        k_cur = lax.dynamic_index_in_dim(kb_b, cb, axis=1, keepdims=False)
        v_cur = lax.dynamic_index_in_dim(vb_b, cb, axis=1, keepdims=False)
        cur_pos = cb * MOBA_BLOCK + jnp.arange(MOBA_BLOCK)
        s_cur = jnp.einsum('qhd,hjd->qhj', qc, k_cur).astype(f32) * scale
        s_cur = jnp.where((cur_pos[None, :] <= t[:, None])[:, None, :], s_cur, NEG_INF)
        if n_sel == 0:
            p_cur = jax.nn.softmax(s_cur, axis=-1).astype(v.dtype)
            return jnp.einsum('qhj,hjd->qhd', p_cur, v_cur)
        gate = jnp.einsum('qhd,hnd->qhn', qc.astype(f32), k_mean[b])
        gate = jnp.where(jnp.arange(nb) < cb, gate, NEG_INF)
        _, idx = lax.top_k(gate, n_sel)
        k_sel = kb_b[h_ix, idx]
        v_sel = vb_b[h_ix, idx]
        s_sel = jnp.einsum('qhd,qhkjd->qhkj', qc, k_sel).astype(f32) * scale
        valid = (jnp.arange(n_sel) < cb)[None, None, :, None]
        s_sel = jnp.where(valid, s_sel, NEG_INF).reshape(Q_BLOCK, H, n_sel * MOBA_BLOCK)
        p = jax.nn.softmax(jnp.concatenate([s_sel, s_cur], axis=-1), axis=-1).astype(v.dtype)
        p_sel = p[..., :n_sel * MOBA_BLOCK].reshape(Q_BLOCK, H, n_sel, MOBA_BLOCK)
        p_cur = p[..., n_sel * MOBA_BLOCK:]
        return (jnp.einsum('qhkj,qhkjd->qhd', p_sel, v_sel)
                + jnp.einsum('qhj,hjd->qhd', p_cur, v_cur))

    out = lax.map(one_block, (b_ids, c_ids, q_blocks))
    return out.reshape(B, S, H, Dh)


def moba_sample(q, k_new, v_new, cache_k, cache_v, page_table, layer):
    B, T, H, Dh = q.shape
    f32 = jnp.float32
    scale = Dh ** -0.5
    page = cache_k.shape[2]
    n_pages = page_table.shape[1]
    past = n_pages * page
    ppb = MOBA_BLOCK // page
    cb = past // MOBA_BLOCK
    n_sel = min(MOBA_TOP_K, cb)
    cur_pages = page_table[:, cb * ppb:]
    n_cur_past = cur_pages.shape[1] * page
    k_cur = jnp.concatenate([cache_k[layer, cur_pages].reshape(B, n_cur_past, H, Dh), k_new], axis=1)
    v_cur = jnp.concatenate([cache_v[layer, cur_pages].reshape(B, n_cur_past, H, Dh), v_new], axis=1)
    cur_mask = jnp.concatenate([jnp.ones((T, n_cur_past), bool), jnp.tril(jnp.ones((T, T), bool))], axis=1)
    s_cur = jnp.einsum('bthd,bjhd->bthj', q, k_cur).astype(f32) * scale
    s_cur = jnp.where(cur_mask[None, :, None, :], s_cur, NEG_INF)
    if n_sel == 0:
        p_cur = jax.nn.softmax(s_cur, axis=-1).astype(v_new.dtype)
        return jnp.einsum('bthj,bjhd->bthd', p_cur, v_cur)

    def block_means(pt_row):
        rows = cache_k[layer, pt_row].astype(f32)
        return jnp.mean(rows.reshape(cb, MOBA_BLOCK, H, Dh), axis=1)
    k_mean = lax.map(block_means, page_table[:, :cb * ppb])
    gate = jnp.einsum('bthd,bnhd->bthn', q.astype(f32), k_mean)
    _, idx = lax.top_k(gate, n_sel)
    phys = page_table[jnp.arange(B)[:, None, None, None, None],
                      idx[..., None] * ppb + jnp.arange(ppb)]
    h_ix = jnp.arange(H)[None, None, :, None, None]
    k_sel = cache_k[layer, phys, :, h_ix, :].reshape(B, T, H, n_sel * MOBA_BLOCK, Dh)
    v_sel = cache_v[layer, phys, :, h_ix, :].reshape(B, T, H, n_sel * MOBA_BLOCK, Dh)
    s_sel = jnp.einsum('bthd,bthjd->bthj', q, k_sel).astype(f32) * scale
    p = jax.nn.softmax(jnp.concatenate([s_sel, s_cur], axis=-1), axis=-1).astype(v_new.dtype)
    p_sel, p_cur = p[..., :n_sel * MOBA_BLOCK], p[..., n_sel * MOBA_BLOCK:]
    return (jnp.einsum('bthj,bthjd->bthd', p_sel, v_sel)
            + jnp.einsum('bthj,bjhd->bthd', p_cur, v_cur))


def mlstm_chunkwise(q, k, v, i_pre, f_pre, c0, n0, m0):
    B, S, H, DK = q.shape
    DV = v.shape[-1]
    f32 = jnp.float32
    L = math.gcd(S, MLSTM_CHUNK)
    nc = S // L
    def chunks(a):
        return jnp.moveaxis(a.astype(f32).reshape((B, nc, L) + a.shape[2:]), 1, 0)
    xs = (chunks(q * DK ** -0.5), chunks(k), chunks(v), chunks(i_pre), chunks(f_pre))
    causal = jnp.tril(jnp.ones((L, L), bool))[None, :, :, None]

    def step(carry, inp):
        c, n, m = carry
        qc, kc, vc, ic, fc = inp
        b = jnp.cumsum(jax.nn.log_sigmoid(fc), axis=1)
        d = jnp.where(causal, b[:, :, None, :] - b[:, None, :, :] + ic[:, None, :, :], NEG_INF)
        inter = b + m[:, None, :]
        m_row = jnp.maximum(inter, jnp.max(d, axis=2))
        w_intra = jnp.exp(d - m_row[:, :, None, :])
        w_inter = jnp.exp(inter - m_row)
        s = jnp.einsum('bshd,brhd->bsrh', qc, kc) * w_intra
        num = jnp.einsum('bsrh,brhv->bshv', s, vc) + w_inter[..., None] * jnp.einsum('bshd,bhdv->bshv', qc, c)
        den = jnp.sum(s, axis=2) + w_inter * jnp.einsum('bshd,bhd->bsh', qc, n)
        h = num / jnp.maximum(jnp.abs(den), jnp.exp(-m_row))[..., None]
        m_new = m_row[:, -1]
        w_state = jnp.exp(b[:, -1:, :] - b + ic - m_new[:, None, :])
        decay = jnp.exp(b[:, -1] + m - m_new)
        c_new = decay[..., None, None] * c + jnp.einsum('brh,brhd,brhv->bhdv', w_state, kc, vc)
        n_new = decay[..., None] * n + jnp.einsum('brh,brhd->bhd', w_state, kc)
        return (c_new, n_new, m_new), h

    (c, n, m), h = lax.scan(step, (c0.astype(f32), n0.astype(f32), m0.astype(f32)), xs)
    h = jnp.moveaxis(h, 0, 1).reshape(B, S, H, DV).astype(v.dtype)
    return h, (c, n, m)


def decoder_layer(x, attend, mlstm_state, lw):
    (ffn1_norm, ffn1_w_gate, ffn1_w_up, ffn1_w_down, mix_norm, w_in, b_if, mlstm_norm,
     w_branch_a, w_branch_b, w_out, ffn2_norm, ffn2_w_gate, ffn2_w_up, ffn2_w_down) = lw
    B, S, _ = x.shape
    x = x + 0.5 * swiglu(rms_norm(x, ffn1_norm), ffn1_w_gate, ffn1_w_up, ffn1_w_down)
    xn = rms_norm(x, mix_norm)
    proj = xn @ w_in
    q_a, k_a, v_a, q_b, k_b, v_b, o_b, if_b, g_a, g_b = jnp.split(proj, SPLIT_POINTS, axis=-1)
    q_a = q_a.reshape(B, S, N_HEADS_A, HEAD_DIM_A)
    k_a = k_a.reshape(B, S, N_HEADS_A, HEAD_DIM_A)
    v_a = v_a.reshape(B, S, N_HEADS_A, HEAD_DIM_A)
    y_a = attend(q_a, k_a, v_a).reshape(B, S, D_A)
    if_b = if_b + b_if
    h_b, new_state = mlstm_chunkwise(q_b.reshape(B, S, N_HEADS_B, DK_B), k_b.reshape(B, S, N_HEADS_B, DK_B),
                                     v_b.reshape(B, S, N_HEADS_B, DV_B), if_b[..., :N_HEADS_B],
                                     if_b[..., N_HEADS_B:], *mlstm_state)
    h_b = rms_norm(h_b, mlstm_norm.reshape(N_HEADS_B, DV_B))
    y_b = jax.nn.sigmoid(o_b) * h_b.reshape(B, S, D_B)
    mixed = jax.nn.sigmoid(g_a) * (y_a @ w_branch_a) + jax.nn.sigmoid(g_b) * (y_b @ w_branch_b)
    x = x + mixed @ w_out
    x = x + 0.5 * swiglu(rms_norm(x, ffn2_norm), ffn2_w_gate, ffn2_w_up, ffn2_w_down)
    return x, k_a, v_a, new_state


def setup_inputs(seed: int = 0) -> dict:
    key = jax.random.key(seed)
    ks = jax.random.split(key, 32)
    f32 = jnp.float32
    n_pages = PAST_LEN // PAGE_SIZE
    n_used = DEC_BATCH * n_pages
    n_pool = n_used + n_used // 4

    def normal(k, shape, scale=1.0):
        return jax.random.normal(k, shape, f32) * scale

    def gain(k, shape):
        return 1.0 + normal(k, shape, 0.02)

    def paged_pool(k):
        pages = lax.map(lambda i: jax.random.normal(jax.random.fold_in(k, i),
                                                    (PAGE_SIZE, N_HEADS_A, HEAD_DIM_A), f32),
                        jnp.arange(DEPTH * n_pool))
        return pages.reshape(DEPTH, n_pool, PAGE_SIZE, N_HEADS_A, HEAD_DIM_A)

    page_table = jax.random.permutation(ks[7], n_pool)[:n_used].reshape(DEC_BATCH, n_pages).astype(jnp.int32)
    b_if = jnp.concatenate([normal(ks[14], (DEPTH, N_HEADS_B), 0.1),
                            FORGET_BIAS + normal(ks[15], (DEPTH, N_HEADS_B), 0.1)], axis=-1)
    return {
        'x_prompt': normal(ks[0], (BATCH, SEQ, D_MODEL)),
        'x_sample': normal(ks[1], (DEC_BATCH, DEC_SEQ, D_MODEL)),
        'cache_k': paged_pool(ks[2]),
        'cache_v': paged_pool(ks[3]),
        'state_c': normal(ks[4], (DEPTH, DEC_BATCH, N_HEADS_B, DK_B, DV_B)),
        'state_n': normal(ks[5], (DEPTH, DEC_BATCH, N_HEADS_B, DK_B), 2.0),
        'state_m': normal(ks[6], (DEPTH, DEC_BATCH, N_HEADS_B)),
        'page_table': page_table,
        'ffn1_norm': gain(ks[8], (DEPTH, D_MODEL)),
        'ffn1_w_gate': normal(ks[9], (DEPTH, D_MODEL, D_FF), D_MODEL ** -0.5),
        'ffn1_w_up': normal(ks[10], (DEPTH, D_MODEL, D_FF), D_MODEL ** -0.5),
        'ffn1_w_down': normal(ks[11], (DEPTH, D_FF, D_MODEL), D_FF ** -0.5),
        'mix_norm': gain(ks[12], (DEPTH, D_MODEL)),
        'w_in': normal(ks[13], (DEPTH, D_MODEL, N_IN), D_MODEL ** -0.5),
        'b_if': b_if,
        'mlstm_norm': gain(ks[16], (DEPTH, D_B)),
        'w_branch_a': normal(ks[17], (DEPTH, D_A, D_MODEL), D_A ** -0.5),
        'w_branch_b': normal(ks[18], (DEPTH, D_B, D_MODEL), D_B ** -0.5),
        'w_out': normal(ks[19], (DEPTH, D_MODEL, D_MODEL), D_MODEL ** -0.5),
        'ffn2_norm': gain(ks[20], (DEPTH, D_MODEL)),
        'ffn2_w_gate': normal(ks[21], (DEPTH, D_MODEL, D_FF), D_MODEL ** -0.5),
        'ffn2_w_up': normal(ks[22], (DEPTH, D_MODEL, D_FF), D_MODEL ** -0.5),
        'ffn2_w_down': normal(ks[23], (DEPTH, D_FF, D_MODEL), D_FF ** -0.5),
        'final_norm': gain(ks[24], (D_MODEL,)),
    }


def reference(x_prompt, x_sample, cache_k, cache_v, state_c, state_n, state_m, page_table,
              ffn1_norm, ffn1_w_gate, ffn1_w_up, ffn1_w_down, mix_norm, w_in, b_if, mlstm_norm,
              w_branch_a, w_branch_b, w_out, ffn2_norm, ffn2_w_gate, ffn2_w_up, ffn2_w_down, final_norm):
    f32 = jnp.float32
    B = x_prompt.shape[0]
    xp, xs = x_prompt, x_sample
    kp_l, vp_l, cp_l, np_l, mp_l = [], [], [], [], []
    ks_l, vs_l, cs_l, ns_l, ms_l = [], [], [], [], []
    for l in range(DEPTH):
        lw = (ffn1_norm[l], ffn1_w_gate[l], ffn1_w_up[l], ffn1_w_down[l], mix_norm[l], w_in[l], b_if[l],
              mlstm_norm[l], w_branch_a[l], w_branch_b[l], w_out[l], ffn2_norm[l], ffn2_w_gate[l],
              ffn2_w_up[l], ffn2_w_down[l])
        st0 = (jnp.zeros((B, N_HEADS_B, DK_B, DV_B), f32), jnp.zeros((B, N_HEADS_B, DK_B), f32),
               jnp.zeros((B, N_HEADS_B), f32))
        xp, kp, vp, (cp, n_p, mp) = decoder_layer(xp, moba_prompt, st0, lw)
        attend_s = functools.partial(moba_sample, cache_k=cache_k, cache_v=cache_v,
                                     page_table=page_table, layer=l)
        xs, k_s, v_s, (cs, n_s, ms) = decoder_layer(xs, attend_s, (state_c[l], state_n[l], state_m[l]), lw)
        kp_l.append(kp); vp_l.append(vp)
        cp_l.append(cp.astype(state_c.dtype)); np_l.append(n_p.astype(state_n.dtype)); mp_l.append(mp.astype(state_m.dtype))
        ks_l.append(k_s); vs_l.append(v_s)
        cs_l.append(cs.astype(state_c.dtype)); ns_l.append(n_s.astype(state_n.dtype)); ms_l.append(ms.astype(state_m.dtype))
    y_prompt = rms_norm(xp, final_norm)
    y_sample = rms_norm(xs, final_norm)
    return (y_prompt, y_sample,
            jnp.stack(kp_l), jnp.stack(vp_l), jnp.stack(cp_l), jnp.stack(np_l), jnp.stack(mp_l),
            jnp.stack(ks_l), jnp.stack(vs_l), jnp.stack(cs_l), jnp.stack(ns_l), jnp.stack(ms_l))
```

```python
import functools

import jax
import jax.numpy as jnp
from jax import lax
from jax.experimental import pallas as pl
from jax.experimental.pallas import tpu as pltpu

F32 = jnp.float32
BF16 = jnp.bfloat16
I32 = jnp.int32

HEAD_DIM_A = 128
MOBA_BLOCK = 256
MOBA_TOP_K = 3
N_HEADS_B = 4
EPS = 1e-6
NEG_INF = -1e30

LANES = 128
SUBLANES = 8
VMEM_LIMIT_BYTES = 56 * 1024 * 1024
FF_TILE = 512
MLSTM_CHUNK_PROMPT = 256
MLSTM_CHUNK_SAMPLE = 128
KMEAN_PAGES_PER_STEP = 16
Q_PAD_ROWS = 16


def _cparams(sem):
    return pltpu.CompilerParams(dimension_semantics=sem, vmem_limit_bytes=VMEM_LIMIT_BYTES)


def _rms(x, g):
    return x * lax.rsqrt(jnp.mean(x * x, axis=-1, keepdims=True) + EPS) * g


def _nt(a, b):
    return lax.dot_general(a, b, (((1,), (1,)), ((), ())), preferred_element_type=F32)


def _tn(a, b):
    return lax.dot_general(a, b, (((0,), (0,)), ((), ())), preferred_element_type=F32)


def _split3(x):
    hi = x.astype(BF16)
    r1 = x - hi.astype(F32)
    mid = r1.astype(BF16)
    lo = (r1 - mid.astype(F32)).astype(BF16)
    return hi, mid, lo


def _ffn_kernel(x_ref, g_ref, wg_ref, wu_ref, wd_ref, go_ref, *rest, final):
    if final:
        y_ref, xn_sc, acc_sc = rest
    else:
        y_ref, yn_ref, xn_sc, acc_sc = rest
    f = pl.program_id(1)

    @pl.when(f == 0)
    def _():
        xn_sc[...] = _rms(x_ref[...], g_ref[...]).astype(BF16)
        acc_sc[...] = jnp.zeros_like(acc_sc)

    xn = xn_sc[...]
    a = jnp.dot(xn, wg_ref[...], preferred_element_type=F32)
    u = jnp.dot(xn, wu_ref[...], preferred_element_type=F32)
    h = (a * jax.nn.sigmoid(a) * u).astype(BF16)
    acc_sc[...] += jnp.dot(h, wd_ref[...], preferred_element_type=F32)

    @pl.when(f == pl.num_programs(1) - 1)
    def _():
        y = x_ref[...] + 0.5 * acc_sc[...]
        if final:
            y_ref[...] = _rms(y, go_ref[...])
        else:
            y_ref[...] = y
            yn_ref[...] = _rms(y, go_ref[...]).astype(BF16)


def _ffn(x, g, wg, wu, wd, g_out, *, tm, final):
    m, d = x.shape
    ff = wg.shape[1]
    grid = (m // tm, ff // FF_TILE)
    row = pl.BlockSpec((tm, d), lambda i, f: (i, 0))
    vec = pl.BlockSpec((1, d), lambda i, f: (0, 0))
    in_specs = [row, vec,
                pl.BlockSpec((d, FF_TILE), lambda i, f: (0, f)),
                pl.BlockSpec((d, FF_TILE), lambda i, f: (0, f)),
                pl.BlockSpec((FF_TILE, d), lambda i, f: (f, 0)),
                vec]
    if final:
        out_shape = jax.ShapeDtypeStruct((m, d), F32)
        out_specs = row
    else:
        out_shape = (jax.ShapeDtypeStruct((m, d), F32), jax.ShapeDtypeStruct((m, d), BF16))
        out_specs = (row, row)
    return pl.pallas_call(
        functools.partial(_ffn_kernel, final=final),
        grid=grid, in_specs=in_specs, out_specs=out_specs, out_shape=out_shape,
        scratch_shapes=[pltpu.VMEM((tm, d), BF16), pltpu.VMEM((tm, d), F32)],
        compiler_params=_cparams(("parallel", "arbitrary")),
        name="ffn_final" if final else "ffn",
    )(x, g.reshape(1, d), wg, wu, wd, g_out.reshape(1, d))


def _mm_kernel(x_ref, w_ref, o_ref, *, act):
    r = jnp.dot(x_ref[...], w_ref[...], preferred_element_type=F32)
    if act == "sigmoid":
        r = jax.nn.sigmoid(r)
    o_ref[...] = r.astype(o_ref.dtype)


def _mm(x, w, *, tm, tn, out_dtype, act=None, name="proj"):
    m, k = x.shape
    n = w.shape[1]
    return pl.pallas_call(
        functools.partial(_mm_kernel, act=act),
        grid=(m // tm, n // tn),
        in_specs=[pl.BlockSpec((tm, k), lambda i, j: (i, 0)),
                  pl.BlockSpec((k, tn), lambda i, j: (0, j))],
        out_specs=pl.BlockSpec((tm, tn), lambda i, j: (i, j)),
        out_shape=jax.ShapeDtypeStruct((m, n), out_dtype),
        compiler_params=_cparams(("parallel", "arbitrary")),
        name=name,
    )(x, w)


def _mm2_kernel(x_ref, w1_ref, w2_ref, o1_ref, o2_ref):
    x = x_ref[...]
    o1_ref[...] = jnp.dot(x, w1_ref[...], preferred_element_type=F32)
    o2_ref[...] = jnp.dot(x, w2_ref[...], preferred_element_type=F32)


def _mm2(x, w1, w2, *, tm, tn):
    m, k = x.shape
    n = w1.shape[1]
    wspec = pl.BlockSpec((k, tn), lambda i, j: (0, j))
    ospec = pl.BlockSpec((tm, tn), lambda i, j: (i, j))
    return pl.pallas_call(
        _mm2_kernel,
        grid=(m // tm, n // tn),
        in_specs=[pl.BlockSpec((tm, k), lambda i, j: (i, 0)), wspec, wspec],
        out_specs=(ospec, ospec),
        out_shape=(jax.ShapeDtypeStruct((m, n), F32), jax.ShapeDtypeStruct((m, n), F32)),
        compiler_params=_cparams(("parallel", "arbitrary")),
        name="proj_kv",
    )(x, w1, w2)


def _merge_kernel(ya_ref, yb_ref, wa_ref, wb_ref, sga_ref, sgb_ref, o_ref):
    a = jnp.dot(ya_ref[...], wa_ref[...], preferred_element_type=F32)
    b = jnp.dot(yb_ref[...], wb_ref[...], preferred_element_type=F32)
    o_ref[...] = (sga_ref[...].astype(F32) * a + sgb_ref[...].astype(F32) * b).astype(o_ref.dtype)


def _merge(ya, yb, wa, wb, gates, *, tm, tn):
    m, k = ya.shape
    n = wa.shape[1]
    nj = n // tn
    xspec = pl.BlockSpec((tm, k), lambda i, j: (i, 0))
    wspec = pl.BlockSpec((k, tn), lambda i, j: (0, j))
    return pl.pallas_call(
        _merge_kernel,
        grid=(m // tm, nj),
        in_specs=[xspec, xspec, wspec, wspec,
                  pl.BlockSpec((tm, tn), lambda i, j: (i, nj + j)),
                  pl.BlockSpec((tm, tn), lambda i, j: (i, 2 * nj + j))],
        out_specs=pl.BlockSpec((tm, tn), lambda i, j: (i, j)),
        out_shape=jax.ShapeDtypeStruct((m, n), BF16),
        compiler_params=_cparams(("parallel", "arbitrary")),
        name="merge",
    )(ya, yb, wa, wb, gates, gates)


def _mm_res_kernel(x_ref, w_ref, r_ref, o_ref):
    o_ref[...] = r_ref[...] + jnp.dot(x_ref[...], w_ref[...], preferred_element_type=F32)


def _mm_res(x, w, res, *, tm, tn):
    m, k = x.shape
    n = w.shape[1]
    return pl.pallas_call(
        _mm_res_kernel,
        grid=(m // tm, n // tn),
        in_specs=[pl.BlockSpec((tm, k), lambda i, j: (i, 0)),
                  pl.BlockSpec((k, tn), lambda i, j: (0, j)),
                  pl.BlockSpec((tm, tn), lambda i, j: (i, j))],
        out_specs=pl.BlockSpec((tm, tn), lambda i, j: (i, j)),
        out_shape=jax.ShapeDtypeStruct((m, n), F32),
        compiler_params=_cparams(("parallel", "arbitrary")),
        name="out_proj",
    )(x, w, res)


def _moba_prompt_kernel(q_ref, k_ref, v_ref, o_ref, *, seq, scale):
    blk = MOBA_BLOCK
    nb = seq // blk
    k32 = k_ref[0]
    kb = k32.astype(BF16)
    vt = v_ref[0].T.astype(BF16)
    kmean = jnp.concatenate(
        [jnp.mean(k32[n * blk:(n + 1) * blk], axis=0, keepdims=True) for n in range(nb)], axis=0)
    if nb % SUBLANES:
        kmean = jnp.concatenate(
            [kmean, jnp.zeros((SUBLANES - nb % SUBLANES, kmean.shape[1]), F32)], axis=0)
    km_hi = kmean.astype(BF16)
    km_lo = (kmean - km_hi.astype(F32)).astype(BF16)
    key_i = lax.broadcasted_iota(I32, (blk, blk), 0)
    qry_i = lax.broadcasted_iota(I32, (blk, blk), 1)
    causal = key_i <= qry_i

    for c in range(nb):
        q = q_ref[0, c * blk:(c + 1) * blk, :]
        blocks = []
        if c > 0:
            sel = None
            if c > MOBA_TOP_K:
                g = _nt(km_hi, q) + _nt(km_lo, q)
                rows = [g[n:n + 1, :] for n in range(c)]
                sel = []
                for n in range(c):
                    rank = jnp.zeros((1, blk), I32)
                    for m in range(c):
                        if m < n:
                            rank += (rows[m] >= rows[n]).astype(I32)
                        elif m > n:
                            rank += (rows[m] > rows[n]).astype(I32)
                    sel.append(rank < MOBA_TOP_K)
            for n in range(c):
                s = _nt(kb[n * blk:(n + 1) * blk], q) * scale
                if sel is not None:
                    s = jnp.where(sel[n], s, NEG_INF)
                blocks.append(s)
        s_cur = _nt(kb[c * blk:(c + 1) * blk], q) * scale
        blocks.append(jnp.where(causal, s_cur, NEG_INF))
        mx = functools.reduce(jnp.maximum, [jnp.max(b, axis=0, keepdims=True) for b in blocks])
        ps = [jnp.exp(b - mx) for b in blocks]
        den = functools.reduce(jnp.add, [jnp.sum(p, axis=0, keepdims=True) for p in ps])
        p_all = jnp.concatenate([p.astype(BF16) for p in ps], axis=0)
        ot = jnp.dot(vt[:, :(c + 1) * blk], p_all, preferred_element_type=F32)
        ot = ot / den
        o_ref[0, c * blk:(c + 1) * blk, :] = ot.T.astype(o_ref.dtype)


def _moba_prompt(q_src, q_col0, k, v, *, batch, seq, heads):
    dh = HEAD_DIM_A
    assert seq % MOBA_BLOCK == 0
    return pl.pallas_call(
        functools.partial(_moba_prompt_kernel, seq=seq, scale=dh ** -0.5),
        grid=(batch, heads),
        in_specs=[pl.BlockSpec((1, seq, dh), lambda b, h: (b, 0, q_col0 + h)),
                  pl.BlockSpec((1, seq, dh), lambda b, h: (b, 0, h)),
                  pl.BlockSpec((1, seq, dh), lambda b, h: (b, 0, h))],
        out_specs=pl.BlockSpec((1, seq, dh), lambda b, h: (b, 0, h)),
        out_shape=jax.ShapeDtypeStruct((batch, seq, heads * dh), BF16),
        compiler_params=_cparams(("parallel", "parallel")),
        name="moba_prompt",
    )(q_src, k, v)


def _mlstm_kernel(q_ref, k_ref, v_ref, og_ref, gr_ref, gb_ref, nw_ref, c0_ref, n0_ref, m0_ref,
                  y_ref, c_ref, n_ref, m_ref, *, chunk, nchunks, dk):
    L = chunk
    r_i = lax.broadcasted_iota(I32, (L, L), 0)
    c_i = lax.broadcasted_iota(I32, (L, L), 1)
    causal = c_i <= r_i
    triu = (r_i <= c_i).astype(BF16)
    row_id = lax.broadcasted_iota(I32, (SUBLANES, L), 0)
    qscale = dk ** -0.5

    c_ref[0, 0] = c0_ref[0, 0]
    n_st = n0_ref[0, 0, 0:1, :]
    m_st = m0_ref[0, 0, 0:1, 0:1]
    nw = nw_ref[0]
    bias = gb_ref[0][:, 0:1]

    for t in range(nchunks):
        rows = slice(t * L, (t + 1) * L)
        g = gr_ref[0, 0, :, rows] + bias
        lf = jnp.minimum(g, 0.0) - jnp.log1p(jnp.exp(-jnp.abs(g)))
        g = jnp.where(row_id == 1, lf, g)
        gh, gm, gl = _split3(g)
        cum = (jnp.dot(gh, triu, preferred_element_type=F32)
               + jnp.dot(gm, triu, preferred_element_type=F32)
               + jnp.dot(gl, triu, preferred_element_type=F32))
        both = jnp.concatenate(
            [g, cum, jnp.zeros((LANES - 2 * SUBLANES, L), F32)], axis=0)
        cols = both.T
        i_col = cols[:, 0:1]
        b_col = cols[:, SUBLANES + 1:SUBLANES + 2]
        i_row = g[0:1, :]
        b_row = cum[1:2, :]

        d = jnp.where(causal, b_col - b_row + i_row, NEG_INF)
        inter = b_col + m_st
        m_row = jnp.maximum(inter, jnp.max(d, axis=1, keepdims=True))
        w_intra = jnp.exp(d - m_row)
        w_inter = jnp.exp(inter - m_row)

        qc = q_ref[0, rows, :] * qscale
        kc = k_ref[0, rows, :]
        vc = v_ref[0, rows, :]
        c_st = c_ref[0, 0]
        s = _nt(qc, kc) * w_intra
        num = (jnp.dot(s.astype(BF16), vc, preferred_element_type=F32)
               + w_inter * jnp.dot(qc, c_st.astype(BF16), preferred_element_type=F32))
        qn = jnp.sum(qc.astype(F32) * n_st, axis=1, keepdims=True)
        den = jnp.sum(s, axis=1, keepdims=True) + w_inter * qn
        h = num / jnp.maximum(jnp.abs(den), jnp.exp(-m_row))
        hn = _rms(h, nw)
        y_ref[0, rows, :] = (og_ref[0, rows, :].astype(F32) * hn).astype(y_ref.dtype)

        m_new = m_row[L - 1:L, :]
        b_last = b_col[L - 1:L, :]
        decay = jnp.exp(b_last + m_st - m_new)
        w_col = jnp.exp(b_last - b_col + i_col - m_new)
        c_ref[0, 0] = decay * c_st + _tn(kc, (vc.astype(F32) * w_col).astype(BF16))
        n_st = decay * n_st + jnp.sum(kc.astype(F32) * w_col, axis=0, keepdims=True)
        m_st = m_new

    n_ref[0, 0] = jnp.broadcast_to(n_st, n_ref.shape[2:])
    m_ref[0, 0] = jnp.broadcast_to(m_st, m_ref.shape[2:])


def _mlstm(qkv_src, og_src, cols, gate_rows, gate_bias, norm_w, c0, n0, m0, *, batch, seq, chunk):
    nh = N_HEADS_B
    dk, dv = c0.shape[2], c0.shape[3]
    q0, k0, v0 = cols[0] // dk, cols[1] // dk, cols[2] // dv
    bh = lambda b, h: (b, h, 0, 0)
    return pl.pallas_call(
        functools.partial(_mlstm_kernel, chunk=chunk, nchunks=seq // chunk, dk=dk),
        grid=(batch, nh),
        in_specs=[pl.BlockSpec((1, seq, dk), lambda b, h: (b, 0, q0 + h)),
                  pl.BlockSpec((1, seq, dk), lambda b, h: (b, 0, k0 + h)),
                  pl.BlockSpec((1, seq, dv), lambda b, h: (b, 0, v0 + h)),
                  pl.BlockSpec((1, seq, dv), lambda b, h: (b, 0, h)),
                  pl.BlockSpec((1, 1, SUBLANES, seq), bh),
                  pl.BlockSpec((1, SUBLANES, LANES), lambda b, h: (h, 0, 0)),
                  pl.BlockSpec((1, 1, dv), lambda b, h: (h, 0, 0)),
                  pl.BlockSpec((1, 1, dk, dv), bh),
                  pl.BlockSpec((1, 1, SUBLANES, dk), bh),
                  pl.BlockSpec((1, 1, SUBLANES, LANES), bh)],
        out_specs=(pl.BlockSpec((1, seq, dv), lambda b, h: (b, 0, h)),
                   pl.BlockSpec((1, 1, dk, dv), bh),
                   pl.BlockSpec((1, 1, SUBLANES, dk), bh),
                   pl.BlockSpec((1, 1, SUBLANES, LANES), bh)),
        out_shape=(jax.ShapeDtypeStruct((batch, seq, nh * dv), BF16),
                   jax.ShapeDtypeStruct((batch, nh, dk, dv), F32),
                   jax.ShapeDtypeStruct((batch, nh, SUBLANES, dk), F32),
                   jax.ShapeDtypeStruct((batch, nh, SUBLANES, LANES), F32)),
        compiler_params=_cparams(("parallel", "parallel")),
        name="mlstm",
    )(qkv_src, qkv_src, qkv_src, og_src, gate_rows, gate_bias, norm_w, c0, n0, m0)


def _kmean_kernel(pt_ref, ck_ref, o_ref, buf, sem, *, steps_per_seq):
    pps, page = buf.shape[1], buf.shape[2]
    ppb = MOBA_BLOCK // page
    step = pl.program_id(0) * steps_per_seq + pl.program_id(1)
    nsteps = pl.num_programs(0) * steps_per_seq

    def copies(step_, slot_):
        b_ = step_ // steps_per_seq
        j_ = step_ % steps_per_seq
        return [pltpu.make_async_copy(ck_ref.at[pt_ref[b_, j_ * pps + i]], buf.at[slot_, i],
                                      sem.at[slot_]) for i in range(pps)]

    slot = step % 2

    @pl.when(step == 0)
    def _():
        for cp in copies(step, slot):
            cp.start()

    @pl.when(step + 1 < nsteps)
    def _():
        for cp in copies(step + 1, 1 - slot):
            cp.start()

    for cp in copies(step, slot):
        cp.wait()

    sums = [jnp.sum(buf[slot, i], axis=0, keepdims=True) for i in range(pps)]
    means = [functools.reduce(jnp.add, sums[i * ppb:(i + 1) * ppb]) * (1.0 / MOBA_BLOCK)
             for i in range(pps // ppb)]
    o_ref[0] = jnp.concatenate(means, axis=0)


def _kmean(cache_k, page_table):
    n_pool, page, width = cache_k.shape
    batch, n_pages = page_table.shape
    pps = KMEAN_PAGES_PER_STEP
    ppb = MOBA_BLOCK // page
    bps = pps // ppb
    assert n_pages % pps == 0 and bps % SUBLANES == 0
    return pl.pallas_call(
        functools.partial(_kmean_kernel, steps_per_seq=n_pages // pps),
        grid_spec=pltpu.PrefetchScalarGridSpec(
            num_scalar_prefetch=1, grid=(batch, n_pages // pps),
            in_specs=[pl.BlockSpec(memory_space=pl.ANY)],
            out_specs=pl.BlockSpec((1, bps, width), lambda b, j, pt: (b, j, 0)),
            scratch_shapes=[pltpu.VMEM((2, pps, page, width), F32),
                            pltpu.SemaphoreType.DMA((2,))]),
        out_shape=jax.ShapeDtypeStruct((batch, n_pages // ppb, width), F32),
        compiler_params=_cparams(("arbitrary", "arbitrary")),
        name="kmean",
    )(page_table, cache_k)


def _topk_kernel(q_ref, km_ref, o_ref, *, heads):
    dh = HEAD_DIM_A
    nblk = km_ref.shape[1]
    lane = lax.broadcasted_iota(I32, (q_ref.shape[1], nblk), 1)
    lane_f = lane.astype(F32)
    out_lane = lax.broadcasted_iota(I32, (q_ref.shape[1], LANES), 1)
    for h in range(heads):
        q3 = _split3(q_ref[0, :, h * dh:(h + 1) * dh])
        k3 = _split3(km_ref[0, :, h * dh:(h + 1) * dh])
        gate = functools.reduce(jnp.add, [_nt(q3[i], k3[j]) for i in range(3) for j in range(3 - i)])
        out = jnp.zeros((q_ref.shape[1], LANES), I32)
        for k in range(MOBA_TOP_K):
            mx = jnp.max(gate, axis=1, keepdims=True)
            idx = jnp.min(jnp.where(gate == mx, lane_f, float(nblk)), axis=1, keepdims=True).astype(I32)
            out = jnp.where(out_lane == k, idx, out)
            gate = jnp.where(lane == idx, -jnp.inf, gate)
        o_ref[0, h] = out


def _topk(q_pad, kmean, *, heads):
    batch, rows, width = q_pad.shape
    nblk = kmean.shape[1]
    return pl.pallas_call(
        functools.partial(_topk_kernel, heads=heads),
        grid=(batch,),
        in_specs=[pl.BlockSpec((1, rows, width), lambda b: (b, 0, 0)),
                  pl.BlockSpec((1, nblk, width), lambda b: (b, 0, 0))],
        out_specs=pl.BlockSpec((1, heads, rows, LANES), lambda b: (b, 0, 0, 0)),
        out_shape=jax.ShapeDtypeStruct((batch, heads, rows, LANES), I32),
        compiler_params=_cparams(("parallel",)),
        name="moba_topk",
    )(q_pad, kmean)


def _moba_sample_kernel(sel_ref, pt_ref, q_ref, kn_ref, vn_ref, ck_ref, cv_ref, o_ref,
                        kbuf, vbuf, sem, *, heads, tokens, scale):
    b = pl.program_id(0)
    h = pl.program_id(1)
    step = b * heads + h
    nsteps = pl.num_programs(0) * heads
    page = kbuf.shape[2]
    ppb = MOBA_BLOCK // page
    nsel = tokens * MOBA_TOP_K

    def copies(step_, slot_):
        b_ = step_ // heads
        h_ = step_ % heads
        out = []
        for j in range(nsel):
            blk = sel_ref[step_ * nsel + j]
            for p in range(ppb):
                phys = pt_ref[b_, blk * ppb + p]
                out.append(pltpu.make_async_copy(
                    ck_ref.at[phys, :, h_, :], kbuf.at[slot_, j * ppb + p], sem.at[0, slot_]))
                out.append(pltpu.make_async_copy(
                    cv_ref.at[phys, :, h_, :], vbuf.at[slot_, j * ppb + p], sem.at[1, slot_]))
        return out

    slot = step % 2

    @pl.when(step == 0)
    def _():
        for cp in copies(step, slot):
            cp.start()

    @pl.when(step + 1 < nsteps)
    def _():
        for cp in copies(step + 1, 1 - slot):
            cp.start()

    for cp in copies(step, slot):
        cp.wait()

    q = q_ref[0]
    qb = q.astype(BF16)
    rows_n = q.shape[0]
    row_id = lax.broadcasted_iota(I32, (rows_n, 1), 0)
    out = jnp.zeros((rows_n, q.shape[1]), F32)
    per_tok = MOBA_TOP_K * ppb
    for t in range(tokens):
        kt = kbuf[slot, t * per_tok:(t + 1) * per_tok].reshape(per_tok * page, -1).astype(BF16)
        vt = vbuf[slot, t * per_tok:(t + 1) * per_tok].reshape(per_tok * page, -1).astype(BF16)
        s_sel = _nt(qb, kt) * scale
        s_new = [jnp.sum(q * kn_ref[0, j:j + 1, :], axis=1, keepdims=True) * scale
                 for j in range(t + 1)]
        mx = functools.reduce(jnp.maximum, s_new + [jnp.max(s_sel, axis=1, keepdims=True)])
        p_sel = jnp.exp(s_sel - mx)
        p_new = [jnp.exp(s - mx) for s in s_new]
        den = functools.reduce(jnp.add, p_new + [jnp.sum(p_sel, axis=1, keepdims=True)])
        o = jnp.dot(p_sel.astype(BF16), vt, preferred_element_type=F32)
        for j in range(t + 1):
            o = o + p_new[j] * vn_ref[0, j:j + 1, :]
        out = jnp.where(row_id == t, o / den, out)
    o_ref[0] = out


def _moba_sample(sel, page_table, q_pad, k_new, v_new, cache_k, cache_v, *, heads, tokens):
    batch, rows, width = q_pad.shape
    dh = HEAD_DIM_A
    n_pool, page = cache_k.shape[0], cache_k.shape[1]
    ppb = MOBA_BLOCK // page
    nbuf = tokens * MOBA_TOP_K * ppb
    blk = pl.BlockSpec((1, rows, dh), lambda b, h, s, p: (b, 0, h))
    return pl.pallas_call(
        functools.partial(_moba_sample_kernel, heads=heads, tokens=tokens, scale=dh ** -0.5),
        grid_spec=pltpu.PrefetchScalarGridSpec(
            num_scalar_prefetch=2, grid=(batch, heads),
            in_specs=[blk, blk, blk,
                      pl.BlockSpec(memory_space=pl.ANY), pl.BlockSpec(memory_space=pl.ANY)],
            out_specs=blk,
            scratch_shapes=[pltpu.VMEM((2, nbuf, page, dh), F32),
                            pltpu.VMEM((2, nbuf, page, dh), F32),
                            pltpu.SemaphoreType.DMA((2, 2))]),
        out_shape=jax.ShapeDtypeStruct((batch, rows, width), F32),
        compiler_params=_cparams(("arbitrary", "arbitrary")),
        name="moba_sample",
    )(sel, page_table, q_pad, k_new, v_new, cache_k, cache_v)


def _pad_cols(w, mult):
    pad = (-w.shape[1]) % mult
    return jnp.pad(w, ((0, 0), (0, pad))) if pad else w


def _pad_rows(w, mult):
    pad = (-w.shape[0]) % mult
    return jnp.pad(w, ((0, pad), (0, 0))) if pad else w


def _prep_weights(ffn1_w_gate, ffn1_w_up, ffn1_w_down, w_in, w_branch_a, w_branch_b, w_out,
                  ffn2_w_gate, ffn2_w_up, ffn2_w_down, *, d_a, d_qk, d_b):
    bf = lambda w: w.astype(BF16)
    o = [0, d_a, 2 * d_a, 3 * d_a, 3 * d_a + d_qk, 3 * d_a + 2 * d_qk, 3 * d_a + 2 * d_qk + d_b,
         3 * d_a + 2 * d_qk + 2 * d_b]
    n_if = 2 * N_HEADS_B
    return dict(
        f1=(_pad_cols(bf(ffn1_w_gate), FF_TILE), _pad_cols(bf(ffn1_w_up), FF_TILE),
            _pad_rows(bf(ffn1_w_down), FF_TILE)),
        f2=(_pad_cols(bf(ffn2_w_gate), FF_TILE), _pad_cols(bf(ffn2_w_up), FF_TILE),
            _pad_rows(bf(ffn2_w_down), FF_TILE)),
        w_k=bf(w_in[:, o[1]:o[2]]), w_v=bf(w_in[:, o[2]:o[3]]),
        w_qkv=bf(jnp.concatenate([w_in[:, o[0]:o[1]], w_in[:, o[3]:o[6]]], axis=1)),
        w_gates=bf(jnp.concatenate([w_in[:, o[6]:o[7]], w_in[:, o[7] + n_if:]], axis=1)),
        w_if=_pad_cols(bf(w_in[:, o[7]:o[7] + n_if]), LANES),
        w_a=bf(w_branch_a), w_b=bf(w_branch_b), w_o=bf(w_out))


def _gate_rows(if_pre, batch, seq, pad_to=None):
    nh = N_HEADS_B
    g = if_pre[:, :2 * nh].reshape(batch, seq, 2, nh)
    if pad_to is not None and pad_to > seq:
        fill = jnp.broadcast_to(jnp.array([NEG_INF, -NEG_INF], F32)[None, None, :, None],
                                (batch, pad_to - seq, 2, nh))
        g = jnp.concatenate([g, fill], axis=1)
    g = g.transpose(0, 3, 2, 1)
    return jnp.pad(g, ((0, 0), (0, 0), (0, SUBLANES - 2), (0, 0)))


def _pad_seq(a, rows):
    return jnp.pad(a, ((0, 0), (0, rows - a.shape[1]), (0, 0)))


def _layer(x, w, norms, b_if, attend, mlstm_in, *, batch, seq, tm):
    ffn1_norm, mix_norm, mlstm_norm, ffn2_norm, final_norm = norms
    d = x.shape[1]
    nh = N_HEADS_B
    tn = 1024
    x1, xn = _ffn(x, ffn1_norm, *w["f1"], mix_norm, tm=tm, final=False)
    k_a, v_a = _mm2(xn, w["w_k"], w["w_v"], tm=tm, tn=tn)
    qkv = _mm(xn, w["w_qkv"], tm=tm, tn=tn, out_dtype=BF16, name="proj_qkv")
    gates = _mm(xn, w["w_gates"], tm=tm, tn=tn, out_dtype=BF16, act="sigmoid", name="proj_gates")
    if_pre = _mm(xn, w["w_if"], tm=tm, tn=LANES, out_dtype=F32, name="proj_if")

    y_a = attend(qkv, k_a, v_a)

    c0, n0, m0, chunk = mlstm_in
    d_a = k_a.shape[1]
    dk, dv = c0.shape[2], c0.shape[3]
    cols = (d_a, d_a + nh * dk, d_a + 2 * nh * dk)
    gb = jnp.zeros((nh, SUBLANES, LANES), F32)
    gb = gb.at[:, 0, :].set(b_if[:nh, None]).at[:, 1, :].set(b_if[nh:, None])
    n0p = jnp.broadcast_to(n0[:, :, None, :], (batch, nh, SUBLANES, dk))
    m0p = jnp.broadcast_to(m0[:, :, None, None], (batch, nh, SUBLANES, LANES))
    if chunk > seq:
        qkv_m = _pad_seq(qkv.reshape(batch, seq, -1), chunk)
        og_m = _pad_seq(gates.reshape(batch, seq, -1), chunk)
        rows = _gate_rows(if_pre, batch, seq, pad_to=chunk)
        sp = chunk
    else:
        qkv_m, og_m = qkv.reshape(batch, seq, -1), gates.reshape(batch, seq, -1)
        rows = _gate_rows(if_pre, batch, seq)
        sp = seq
    y_b, c, n, m = _mlstm(qkv_m, og_m, cols, rows, gb, mlstm_norm.reshape(nh, 1, dv), c0, n0p, m0p,
                          batch=batch, seq=sp, chunk=chunk)
    y_b = y_b[:, :seq].reshape(batch * seq, nh * dv)

    mixed = _merge(y_a, y_b, w["w_a"], w["w_b"], gates, tm=tm, tn=tn)
    x2 = _mm_res(mixed, w["w_o"], x1, tm=tm, tn=tn)
    y = _ffn(x2, ffn2_norm, *w["f2"], final_norm, tm=tm, final=True)
    return y, k_a, v_a, (c, n[:, :, 0, :], m[:, :, 0, 0])


def kernel(x_prompt, x_sample, cache_k, cache_v, state_c, state_n, state_m, page_table, ffn1_norm, ffn1_w_gate, ffn1_w_up, ffn1_w_down, mix_norm, w_in, b_if, mlstm_norm, w_branch_a, w_branch_b, w_out, ffn2_norm, ffn2_w_gate, ffn2_w_up, ffn2_w_down, final_norm):
    depth = ffn1_norm.shape[0]
    assert depth == 1, "single-layer step"
    B, S, D = x_prompt.shape
    Bs, T, _ = x_sample.shape
    dh = HEAD_DIM_A
    nh_b = N_HEADS_B
    dk, dv = state_c.shape[3], state_c.shape[4]
    d_a = w_branch_a.shape[1]
    heads = d_a // dh
    n_pool, page = cache_k.shape[1], cache_k.shape[2]
    n_pages = page_table.shape[1]
    assert (n_pages * page) % MOBA_BLOCK == 0 and (n_pages * page) // MOBA_BLOCK >= MOBA_TOP_K
    l = 0

    w = _prep_weights(ffn1_w_gate[l], ffn1_w_up[l], ffn1_w_down[l], w_in[l], w_branch_a[l],
                      w_branch_b[l], w_out[l], ffn2_w_gate[l], ffn2_w_up[l], ffn2_w_down[l],
                      d_a=d_a, d_qk=nh_b * dk, d_b=nh_b * dv)
    norms = (ffn1_norm[l], mix_norm[l], mlstm_norm[l], ffn2_norm[l], final_norm)

    def attend_p(qkv, k_a, v_a):
        o = _moba_prompt(qkv.reshape(B, S, -1), 0, k_a.reshape(B, S, d_a), v_a.reshape(B, S, d_a),
                         batch=B, seq=S, heads=heads)
        return o.reshape(B * S, d_a)

    st0 = (jnp.zeros((B, nh_b, dk, dv), F32), jnp.zeros((B, nh_b, dk), F32),
           jnp.zeros((B, nh_b), F32), MLSTM_CHUNK_PROMPT)
    yp, kp, vp, (cp, n_p, mp) = _layer(x_prompt.reshape(B * S, D), w, norms, b_if[l], attend_p, st0,
                                       batch=B, seq=S, tm=512)

    ck = cache_k[l]
    cv = cache_v[l]

    def attend_s(qkv, k_a, v_a):
        q = qkv[:, :d_a].astype(F32).reshape(Bs, T, d_a)
        q_pad = _pad_seq(q, Q_PAD_ROWS)
        kmean = _kmean(ck.reshape(n_pool, page, d_a), page_table)
        top = _topk(q_pad, kmean, heads=heads)
        sel = top[:, :, :T, :MOBA_TOP_K].reshape(-1)
        o = _moba_sample(sel, page_table, q_pad, _pad_seq(k_a.reshape(Bs, T, d_a), Q_PAD_ROWS),
                         _pad_seq(v_a.reshape(Bs, T, d_a), Q_PAD_ROWS), ck, cv, heads=heads, tokens=T)
        return o[:, :T].reshape(Bs * T, d_a).astype(BF16)

    st_s = (state_c[l], state_n[l], state_m[l], MLSTM_CHUNK_SAMPLE)
    ys, k_s, v_s, (cs, n_s, ms) = _layer(x_sample.reshape(Bs * T, D), w, norms, b_if[l], attend_s, st_s,
                                         batch=Bs, seq=T, tm=Bs * T)

    return (yp.reshape(B, S, D), ys.reshape(Bs, T, D),
            kp.reshape(1, B, S, heads, dh), vp.reshape(1, B, S, heads, dh),
            cp[None], n_p[None], mp[None],
            k_s.reshape(1, Bs, T, heads, dh), v_s.reshape(1, Bs, T, heads, dh),
            cs[None], n_s[None], ms[None])
```

```python
import functools

import jax
import jax.numpy as jnp
from jax import lax
from jax.experimental import pallas as pl
from jax.experimental.pallas import tpu as pltpu

F32 = jnp.float32
BF16 = jnp.bfloat16
I32 = jnp.int32

HEAD_DIM_A = 128
MOBA_BLOCK = 256
MOBA_TOP_K = 3
N_HEADS_B = 4
EPS = 1e-6
NEG_INF = -1e30

LANES = 128
SUBLANES = 8
BF16_ROWS = 16
QK_LOG2_SCALE = HEAD_DIM_A ** -0.5 * 1.4426950408889634
VMEM_LIMIT_BYTES = 56 * 1024 * 1024
FF_TILE = 512
MLSTM_CHUNK_PROMPT = 256
MLSTM_CHUNK_SAMPLE = 128
SAMPLE_HEADS_PER_STEP = 4
Q_PAD_ROWS = BF16_ROWS


def _cparams(sem):
    return pltpu.CompilerParams(dimension_semantics=sem, vmem_limit_bytes=VMEM_LIMIT_BYTES)


def _rms(x, g):
    return x * lax.rsqrt(jnp.mean(x * x, axis=-1, keepdims=True) + EPS) * g


def _nt(a, b):
    return lax.dot_general(a, b, (((1,), (1,)), ((), ())), preferred_element_type=F32)


def _tn(a, b):
    return lax.dot_general(a, b, (((0,), (0,)), ((), ())), preferred_element_type=F32)


def _split3(x):
    hi = x.astype(BF16)
    r1 = x - hi.astype(F32)
    mid = r1.astype(BF16)
    lo = (r1 - mid.astype(F32)).astype(BF16)
    return hi, mid, lo


def _ffn_kernel(x_ref, g_ref, wg_ref, wu_ref, wd_ref, go_ref, *rest, final):
    if final:
        y_ref, xn_sc, acc_sc = rest
    else:
        y_ref, yn_ref, xn_sc, acc_sc = rest
    f = pl.program_id(1)

    @pl.when(f == 0)
    def _():
        xn_sc[...] = _rms(x_ref[...], g_ref[...]).astype(BF16)
        acc_sc[...] = jnp.zeros_like(acc_sc)

    xn = xn_sc[...]
    a = jnp.dot(xn, wg_ref[...], preferred_element_type=F32)
    u = jnp.dot(xn, wu_ref[...], preferred_element_type=F32)
    h = (a * jax.nn.sigmoid(a) * u).astype(BF16)
    acc_sc[...] += jnp.dot(h, wd_ref[...], preferred_element_type=F32)

    @pl.when(f == pl.num_programs(1) - 1)
    def _():
        y = x_ref[...] + 0.5 * acc_sc[...]
        if final:
            y_ref[...] = _rms(y, go_ref[...])
        else:
            y_ref[...] = y
            yn_ref[...] = _rms(y, go_ref[...]).astype(BF16)


def _ffn(x, g, wg, wu, wd, g_out, *, tm, final):
    m, d = x.shape
    ff = wg.shape[1]
    grid = (m // tm, ff // FF_TILE)
    row = pl.BlockSpec((tm, d), lambda i, f: (i, 0))
    vec = pl.BlockSpec((1, d), lambda i, f: (0, 0))
    in_specs = [row, vec,
                pl.BlockSpec((d, FF_TILE), lambda i, f: (0, f)),
                pl.BlockSpec((d, FF_TILE), lambda i, f: (0, f)),
                pl.BlockSpec((FF_TILE, d), lambda i, f: (f, 0)),
                vec]
    if final:
        out_shape = jax.ShapeDtypeStruct((m, d), F32)
        out_specs = row
    else:
        out_shape = (jax.ShapeDtypeStruct((m, d), F32), jax.ShapeDtypeStruct((m, d), BF16))
        out_specs = (row, row)
    return pl.pallas_call(
        functools.partial(_ffn_kernel, final=final),
        grid=grid, in_specs=in_specs, out_specs=out_specs, out_shape=out_shape,
        scratch_shapes=[pltpu.VMEM((tm, d), BF16), pltpu.VMEM((tm, d), F32)],
        compiler_params=_cparams(("parallel", "arbitrary")),
        name="ffn_final" if final else "ffn",
    )(x, g.reshape(1, d), wg, wu, wd, g_out.reshape(1, d))


def _mm_kernel(x_ref, w_ref, o_ref, *, act, lead_tiles, lead_scale):
    r = jnp.dot(x_ref[...], w_ref[...], preferred_element_type=F32)
    if act == "sigmoid":
        r = jax.nn.sigmoid(r)
    if lead_tiles:
        r = r * jnp.where(pl.program_id(0) < lead_tiles, lead_scale, 1.0)
    o_ref[...] = r.astype(o_ref.dtype)


def _mm(x, w, *, tm, tn, out_dtype, act=None, lead_cols=0, lead_scale=1.0, name="proj"):
    m, k = x.shape
    n = w.shape[1]
    assert lead_cols % tn == 0
    return pl.pallas_call(
        functools.partial(_mm_kernel, act=act, lead_tiles=lead_cols // tn, lead_scale=lead_scale),
        grid=(n // tn, m // tm),
        in_specs=[pl.BlockSpec((tm, k), lambda j, i: (i, 0)),
                  pl.BlockSpec((k, tn), lambda j, i: (0, j))],
        out_specs=pl.BlockSpec((tm, tn), lambda j, i: (i, j)),
        out_shape=jax.ShapeDtypeStruct((m, n), out_dtype),
        compiler_params=_cparams(("parallel", "arbitrary")),
        name=name,
    )(x, w)


def _mm2_kernel(x_ref, w1_ref, w2_ref, o1_ref, o2_ref):
    x = x_ref[...]
    o1_ref[...] = jnp.dot(x, w1_ref[...], preferred_element_type=F32)
    o2_ref[...] = jnp.dot(x, w2_ref[...], preferred_element_type=F32)


def _mm2(x, w1, w2, *, tm, tn):
    m, k = x.shape
    n = w1.shape[1]
    wspec = pl.BlockSpec((k, tn), lambda j, i: (0, j))
    ospec = pl.BlockSpec((tm, tn), lambda j, i: (i, j))
    return pl.pallas_call(
        _mm2_kernel,
        grid=(n // tn, m // tm),
        in_specs=[pl.BlockSpec((tm, k), lambda j, i: (i, 0)), wspec, wspec],
        out_specs=(ospec, ospec),
        out_shape=(jax.ShapeDtypeStruct((m, n), F32), jax.ShapeDtypeStruct((m, n), F32)),
        compiler_params=_cparams(("parallel", "arbitrary")),
        name="proj_kv",
    )(x, w1, w2)


def _merge_kernel(ya_ref, yb_ref, wa_ref, wb_ref, sga_ref, sgb_ref, o_ref):
    a = jnp.dot(ya_ref[...], wa_ref[...], preferred_element_type=F32)
    b = jnp.dot(yb_ref[...], wb_ref[...], preferred_element_type=F32)
    o_ref[...] = (sga_ref[...].astype(F32) * a + sgb_ref[...].astype(F32) * b).astype(o_ref.dtype)


def _merge(ya, yb, wa, wb, gates, *, tm, tn):
    m, k = ya.shape
    n = wa.shape[1]
    nj = n // tn
    xspec = pl.BlockSpec((tm, k), lambda j, i: (i, 0))
    wspec = pl.BlockSpec((k, tn), lambda j, i: (0, j))
    return pl.pallas_call(
        _merge_kernel,
        grid=(nj, m // tm),
        in_specs=[xspec, xspec, wspec, wspec,
                  pl.BlockSpec((tm, tn), lambda j, i: (i, nj + j)),
                  pl.BlockSpec((tm, tn), lambda j, i: (i, 2 * nj + j))],
        out_specs=pl.BlockSpec((tm, tn), lambda j, i: (i, j)),
        out_shape=jax.ShapeDtypeStruct((m, n), BF16),
        compiler_params=_cparams(("parallel", "arbitrary")),
        name="merge",
    )(ya, yb, wa, wb, gates, gates)


def _mm_res_kernel(x_ref, w_ref, r_ref, o_ref):
    o_ref[...] = r_ref[...] + jnp.dot(x_ref[...], w_ref[...], preferred_element_type=F32)


def _mm_res(x, w, res, *, tm, tn):
    m, k = x.shape
    n = w.shape[1]
    return pl.pallas_call(
        _mm_res_kernel,
        grid=(n // tn, m // tm),
        in_specs=[pl.BlockSpec((tm, k), lambda j, i: (i, 0)),
                  pl.BlockSpec((k, tn), lambda j, i: (0, j)),
                  pl.BlockSpec((tm, tn), lambda j, i: (i, j))],
        out_specs=pl.BlockSpec((tm, tn), lambda j, i: (i, j)),
        out_shape=jax.ShapeDtypeStruct((m, n), F32),
        compiler_params=_cparams(("parallel", "arbitrary")),
        name="out_proj",
    )(x, w, res)


def _page_block_means(pt_ref, ck_ref, km_ref, buf, sem, *, steps_per_seq):
    pps, page = buf.shape[1], buf.shape[2]
    ppb = MOBA_BLOCK // page
    step = pl.program_id(0) * pl.num_programs(1) + pl.program_id(1)
    nsteps = pl.num_programs(0) * pl.num_programs(1)

    def copies(step_, slot_):
        b_ = step_ // steps_per_seq
        j_ = step_ % steps_per_seq
        return [pltpu.make_async_copy(ck_ref.at[pt_ref[b_, j_ * pps + i]], buf.at[slot_, i],
                                      sem.at[slot_]) for i in range(pps)]

    slot = step % 2

    @pl.when(step == 0)
    def _():
        for cp in copies(step, slot):
            cp.start()

    @pl.when(step + 1 < nsteps)
    def _():
        for cp in copies(step + 1, 1 - slot):
            cp.start()

    for cp in copies(step, slot):
        cp.wait()

    for i in range(pps // ppb):
        tot = functools.reduce(jnp.add, [jnp.sum(buf[slot, i * ppb + p], axis=0) for p in range(ppb)])
        km_ref[0, i] = tot * (1.0 / MOBA_BLOCK)


def _moba_prompt_kernel(pt_ref, q_ref, k_ref, v_ref, ck_ref, o_ref, km_ref, buf, sem, *,
                        seq, steps_per_seq):
    _page_block_means(pt_ref, ck_ref, km_ref, buf, sem, steps_per_seq=steps_per_seq)
    blk = MOBA_BLOCK
    nb = seq // blk
    dh = k_ref.shape[2]
    k32 = k_ref[0]
    kb = k32.astype(BF16)
    vt = jnp.concatenate([v_ref[0].T.astype(BF16), jnp.ones((BF16_ROWS, seq), BF16)], axis=0)
    kmean = jnp.concatenate(
        [jnp.mean(k32[n * blk:(n + 1) * blk], axis=0, keepdims=True) for n in range(nb)], axis=0)
    if nb % SUBLANES:
        kmean = jnp.concatenate(
            [kmean, jnp.zeros((SUBLANES - nb % SUBLANES, kmean.shape[1]), F32)], axis=0)
    km_hi = kmean.astype(BF16)
    km_lo = (kmean - km_hi.astype(F32)).astype(BF16)
    key_i = lax.broadcasted_iota(I32, (blk, blk), 0)
    qry_i = lax.broadcasted_iota(I32, (blk, blk), 1)
    causal = key_i <= qry_i

    for c in range(nb):
        q = q_ref[0, c * blk:(c + 1) * blk, :]
        blocks = []
        if c > 0:
            sel = None
            if c > MOBA_TOP_K:
                g = _nt(km_hi, q) + _nt(km_lo, q)
                rows = [g[n:n + 1, :] for n in range(c)]
                sel = []
                for n in range(c):
                    rank = jnp.zeros((1, blk), I32)
                    for m in range(c):
                        if m < n:
                            rank += (rows[m] >= rows[n]).astype(I32)
                        elif m > n:
                            rank += (rows[m] > rows[n]).astype(I32)
                    sel.append(rank < MOBA_TOP_K)
            for n in range(c):
                s = _nt(kb[n * blk:(n + 1) * blk], q)
                if sel is not None:
                    s = jnp.where(sel[n], s, NEG_INF)
                blocks.append(s)
        s_cur = _nt(kb[c * blk:(c + 1) * blk], q)
        blocks.append(jnp.where(causal, s_cur, NEG_INF))
        mx = functools.reduce(jnp.maximum, [jnp.max(b, axis=0, keepdims=True) for b in blocks])
        p_all = jnp.concatenate([jnp.exp2(b - mx).astype(BF16) for b in blocks], axis=0)
        ot = jnp.dot(vt[:, :(c + 1) * blk], p_all, preferred_element_type=F32)
        ot = ot[:dh] / ot[dh:dh + 1]
        o_ref[0, c * blk:(c + 1) * blk, :] = ot.T.astype(o_ref.dtype)


def _moba_prompt(q_src, q_col0, k, v, cache_k, page_table, *, batch, seq, heads):
    dh = HEAD_DIM_A
    assert seq % MOBA_BLOCK == 0
    n_pool, page, c_heads, c_dh = cache_k.shape
    s_batch, n_pages = page_table.shape
    ppb = MOBA_BLOCK // page
    nsteps = batch * heads
    assert (s_batch * n_pages) % nsteps == 0
    pps = s_batch * n_pages // nsteps
    assert pps % ppb == 0 and n_pages % pps == 0
    steps_per_seq = n_pages // pps
    bps = pps // ppb

    def km_index(b, h, pt):
        step = b * heads + h
        return (step // steps_per_seq, step % steps_per_seq, 0, 0)

    return pl.pallas_call(
        functools.partial(_moba_prompt_kernel, seq=seq, steps_per_seq=steps_per_seq),
        grid_spec=pltpu.PrefetchScalarGridSpec(
            num_scalar_prefetch=1, grid=(batch, heads),
            in_specs=[pl.BlockSpec((1, seq, dh), lambda b, h, pt: (b, 0, q_col0 + h)),
                      pl.BlockSpec((1, seq, dh), lambda b, h, pt: (b, 0, h)),
                      pl.BlockSpec((1, seq, dh), lambda b, h, pt: (b, 0, h)),
                      pl.BlockSpec(memory_space=pl.ANY)],
            out_specs=(pl.BlockSpec((1, seq, dh), lambda b, h, pt: (b, 0, h)),
                       pl.BlockSpec((1, bps, c_heads, c_dh), km_index)),
            scratch_shapes=[pltpu.VMEM((2, pps, page, c_heads, c_dh), F32),
                            pltpu.SemaphoreType.DMA((2,))]),
        out_shape=(jax.ShapeDtypeStruct((batch, seq, heads * dh), BF16),
                   jax.ShapeDtypeStruct((s_batch, n_pages // ppb, c_heads, c_dh), F32)),
        compiler_params=_cparams(("arbitrary", "arbitrary")),
        name="moba_prompt",
    )(page_table, q_src, k, v, cache_k)


def _mlstm_kernel(q_ref, k_ref, v_ref, og_ref, gr_ref, gb_ref, nw_ref, c0_ref, n0_ref, m0_ref,
                  y_ref, c_ref, n_ref, m_ref, *, chunk, nchunks, dk):
    L = chunk
    r_i = lax.broadcasted_iota(I32, (L, L), 0)
    c_i = lax.broadcasted_iota(I32, (L, L), 1)
    causal = c_i <= r_i
    triu = (r_i <= c_i).astype(BF16)
    row_id = lax.broadcasted_iota(I32, (SUBLANES, L), 0)
    qscale = dk ** -0.5

    c_ref[0, 0] = c0_ref[0, 0]
    n_st = n0_ref[0, 0, 0:1, :]
    m_st = m0_ref[0, 0, 0:1, 0:1]
    nw = nw_ref[0]
    bias = gb_ref[0][:, 0:1]

    for t in range(nchunks):
        rows = slice(t * L, (t + 1) * L)
        g = gr_ref[0, 0, :, rows] + bias
        lf = jnp.minimum(g, 0.0) - jnp.log1p(jnp.exp(-jnp.abs(g)))
        g = jnp.where(row_id == 1, lf, g)
        gh, gm, gl = _split3(g)
        cum = (jnp.dot(gh, triu, preferred_element_type=F32)
               + jnp.dot(gm, triu, preferred_element_type=F32)
               + jnp.dot(gl, triu, preferred_element_type=F32))
        both = jnp.concatenate(
            [g, cum, jnp.zeros((LANES - 2 * SUBLANES, L), F32)], axis=0)
        cols = both.T
        i_col = cols[:, 0:1]
        b_col = cols[:, SUBLANES + 1:SUBLANES + 2]
        i_row = g[0:1, :]
        b_row = cum[1:2, :]

        d = jnp.where(causal, b_col - b_row + i_row, NEG_INF)
        inter = b_col + m_st
        m_row = jnp.maximum(inter, jnp.max(d, axis=1, keepdims=True))
        w_intra = jnp.exp(d - m_row)
        w_inter = jnp.exp(inter - m_row)

        qc = q_ref[0, rows, :] * qscale
        kc = k_ref[0, rows, :]
        vc = v_ref[0, rows, :]
        c_st = c_ref[0, 0]
        s = _nt(qc, kc) * w_intra
        num = (jnp.dot(s.astype(BF16), vc, preferred_element_type=F32)
               + w_inter * jnp.dot(qc, c_st.astype(BF16), preferred_element_type=F32))
        qn = jnp.sum(qc.astype(F32) * n_st, axis=1, keepdims=True)
        den = jnp.sum(s, axis=1, keepdims=True) + w_inter * qn
        h = num / jnp.maximum(jnp.abs(den), jnp.exp(-m_row))
        hn = _rms(h, nw)
        y_ref[0, rows, :] = (og_ref[0, rows, :].astype(F32) * hn).astype(y_ref.dtype)

        m_new = m_row[L - 1:L, :]
        b_last = b_col[L - 1:L, :]
        decay = jnp.exp(b_last + m_st - m_new)
        w_col = jnp.exp(b_last - b_col + i_col - m_new)
        c_ref[0, 0] = decay * c_st + _tn(kc, (vc.astype(F32) * w_col).astype(BF16))
        n_st = decay * n_st + jnp.sum(kc.astype(F32) * w_col, axis=0, keepdims=True)
        m_st = m_new

    n_ref[0, 0] = jnp.broadcast_to(n_st, n_ref.shape[2:])
    m_ref[0, 0] = jnp.broadcast_to(m_st, m_ref.shape[2:])


def _mlstm(qkv_src, og_src, cols, gate_rows, gate_bias, norm_w, c0, n0, m0, *, batch, seq, chunk):
    nh = N_HEADS_B
    dk, dv = c0.shape[2], c0.shape[3]
    q0, k0, v0 = cols[0] // dk, cols[1] // dk, cols[2] // dv
    bh = lambda b, h: (b, h, 0, 0)
    return pl.pallas_call(
        functools.partial(_mlstm_kernel, chunk=chunk, nchunks=seq // chunk, dk=dk),
        grid=(batch, nh),
        in_specs=[pl.BlockSpec((1, seq, dk), lambda b, h: (b, 0, q0 + h)),
                  pl.BlockSpec((1, seq, dk), lambda b, h: (b, 0, k0 + h)),
                  pl.BlockSpec((1, seq, dv), lambda b, h: (b, 0, v0 + h)),
                  pl.BlockSpec((1, seq, dv), lambda b, h: (b, 0, h)),
                  pl.BlockSpec((1, 1, SUBLANES, seq), bh),
                  pl.BlockSpec((1, SUBLANES, LANES), lambda b, h: (h, 0, 0)),
                  pl.BlockSpec((1, 1, dv), lambda b, h: (h, 0, 0)),
                  pl.BlockSpec((1, 1, dk, dv), bh),
                  pl.BlockSpec((1, 1, SUBLANES, dk), bh),
                  pl.BlockSpec((1, 1, SUBLANES, LANES), bh)],
        out_specs=(pl.BlockSpec((1, seq, dv), lambda b, h: (b, 0, h)),
                   pl.BlockSpec((1, 1, dk, dv), bh),
                   pl.BlockSpec((1, 1, SUBLANES, dk), bh),
                   pl.BlockSpec((1, 1, SUBLANES, LANES), bh)),
        out_shape=(jax.ShapeDtypeStruct((batch, seq, nh * dv), BF16),
                   jax.ShapeDtypeStruct((batch, nh, dk, dv), F32),
                   jax.ShapeDtypeStruct((batch, nh, SUBLANES, dk), F32),
                   jax.ShapeDtypeStruct((batch, nh, SUBLANES, LANES), F32)),
        compiler_params=_cparams(("parallel", "parallel")),
        name="mlstm",
    )(qkv_src, qkv_src, qkv_src, og_src, gate_rows, gate_bias, norm_w, c0, n0, m0)


def _topk_kernel(q_ref, km_ref, o_ref, *, heads):
    dh = HEAD_DIM_A
    nblk = km_ref.shape[1]
    lane = lax.broadcasted_iota(I32, (q_ref.shape[1], nblk), 1)
    lane_f = lane.astype(F32)
    out_lane = lax.broadcasted_iota(I32, (q_ref.shape[1], LANES), 1)
    for h in range(heads):
        q3 = _split3(q_ref[0, :, h * dh:(h + 1) * dh])
        k3 = _split3(km_ref[0, :, h, :])
        gate = functools.reduce(jnp.add, [_nt(q3[i], k3[j]) for i in range(3) for j in range(3 - i)])
        out = jnp.zeros((q_ref.shape[1], LANES), I32)
        for k in range(MOBA_TOP_K):
            mx = jnp.max(gate, axis=1, keepdims=True)
            idx = jnp.min(jnp.where(gate == mx, lane_f, float(nblk)), axis=1, keepdims=True).astype(I32)
            out = jnp.where(out_lane == k, idx, out)
            gate = jnp.where(lane == idx, -jnp.inf, gate)
        o_ref[0, h] = out


def _topk(q_pad, kmean, *, heads):
    batch, rows, width = q_pad.shape
    nblk = kmean.shape[1]
    return pl.pallas_call(
        functools.partial(_topk_kernel, heads=heads),
        grid=(batch,),
        in_specs=[pl.BlockSpec((1, rows, width), lambda b: (b, 0, 0)),
                  pl.BlockSpec((1, nblk, heads, HEAD_DIM_A), lambda b: (b, 0, 0, 0))],
        out_specs=pl.BlockSpec((1, heads, rows, LANES), lambda b: (b, 0, 0, 0)),
        out_shape=jax.ShapeDtypeStruct((batch, heads, rows, LANES), I32),
        compiler_params=_cparams(("parallel",)),
        name="moba_topk",
    )(q_pad, kmean)


def _moba_sample_kernel(sel_ref, pt_ref, q_ref, kn_ref, vn_ref, ck_ref, cv_ref, o_ref,
                        kbuf, vbuf, sem, *, heads, tokens):
    hps = SAMPLE_HEADS_PER_STEP
    groups = heads // hps
    step = pl.program_id(0) * groups + pl.program_id(1)
    nsteps = pl.num_programs(0) * groups
    page, dh = kbuf.shape[2], kbuf.shape[3]
    ppb = MOBA_BLOCK // page
    nsel = tokens * MOBA_TOP_K
    per_head = nsel * ppb

    def copies(step_, slot_):
        b_ = step_ // groups
        out = []
        for hh in range(hps):
            h_ = (step_ % groups) * hps + hh
            for j in range(nsel):
                blk = sel_ref[(b_ * heads + h_) * nsel + j]
                for p in range(ppb):
                    phys = pt_ref[b_, blk * ppb + p]
                    dst = hh * per_head + j * ppb + p
                    out.append(pltpu.make_async_copy(
                        ck_ref.at[phys, :, h_, :], kbuf.at[slot_, dst], sem.at[0, slot_]))
                    out.append(pltpu.make_async_copy(
                        cv_ref.at[phys, :, h_, :], vbuf.at[slot_, dst], sem.at[1, slot_]))
        return out

    slot = step % 2

    @pl.when(step == 0)
    def _():
        for cp in copies(step, slot):
            cp.start()

    @pl.when(step + 1 < nsteps)
    def _():
        for cp in copies(step + 1, 1 - slot):
            cp.start()

    for cp in copies(step, slot):
        cp.wait()

    rows_n = q_ref.shape[1]
    row_id = lax.broadcasted_iota(I32, (rows_n, 1), 0)
    per_tok = MOBA_TOP_K * ppb
    for hh in range(hps):
        lanes = slice(hh * dh, (hh + 1) * dh)
        q = q_ref[0, :, lanes]
        qb = q.astype(BF16)
        out = jnp.zeros((rows_n, dh), F32)
        for t in range(tokens):
            pages = slice(hh * per_head + t * per_tok, hh * per_head + (t + 1) * per_tok)
            kt = kbuf[slot, pages].reshape(per_tok * page, dh).astype(BF16)
            vt = vbuf[slot, pages].reshape(per_tok * page, dh).astype(BF16)
            s_sel = _nt(qb, kt)
            s_new = [jnp.sum(q * kn_ref[0, j:j + 1, lanes], axis=1, keepdims=True)
                     for j in range(t + 1)]
            mx = functools.reduce(jnp.maximum, s_new + [jnp.max(s_sel, axis=1, keepdims=True)])
            p_sel = jnp.exp2(s_sel - mx)
            p_new = [jnp.exp2(s - mx) for s in s_new]
            den = functools.reduce(jnp.add, p_new + [jnp.sum(p_sel, axis=1, keepdims=True)])
            o = jnp.dot(p_sel.astype(BF16), vt, preferred_element_type=F32)
            for j in range(t + 1):
                o = o + p_new[j] * vn_ref[0, j:j + 1, lanes]
            out = jnp.where(row_id == t, o / den, out)
        o_ref[0, :, lanes] = out


def _moba_sample(sel, page_table, q_pad, k_new, v_new, cache_k, cache_v, *, heads, tokens):
    batch, rows, width = q_pad.shape
    dh = HEAD_DIM_A
    n_pool, page = cache_k.shape[0], cache_k.shape[1]
    ppb = MOBA_BLOCK // page
    hps = SAMPLE_HEADS_PER_STEP
    assert heads % hps == 0
    nbuf = hps * tokens * MOBA_TOP_K * ppb
    blk = pl.BlockSpec((1, rows, hps * dh), lambda b, g, s, p: (b, 0, g))
    return pl.pallas_call(
        functools.partial(_moba_sample_kernel, heads=heads, tokens=tokens),
        grid_spec=pltpu.PrefetchScalarGridSpec(
            num_scalar_prefetch=2, grid=(batch, heads // hps),
            in_specs=[blk, blk, blk,
                      pl.BlockSpec(memory_space=pl.ANY), pl.BlockSpec(memory_space=pl.ANY)],
            out_specs=blk,
            scratch_shapes=[pltpu.VMEM((2, nbuf, page, dh), F32),
                            pltpu.VMEM((2, nbuf, page, dh), F32),
                            pltpu.SemaphoreType.DMA((2, 2))]),
        out_shape=jax.ShapeDtypeStruct((batch, rows, width), F32),
        compiler_params=_cparams(("arbitrary", "arbitrary")),
        name="moba_sample",
    )(sel, page_table, q_pad, k_new, v_new, cache_k, cache_v)


def _pad_cols(w, mult):
    pad = (-w.shape[1]) % mult
    return jnp.pad(w, ((0, 0), (0, pad))) if pad else w


def _pad_rows(w, mult):
    pad = (-w.shape[0]) % mult
    return jnp.pad(w, ((0, pad), (0, 0))) if pad else w


def _prep_weights(ffn1_w_gate, ffn1_w_up, ffn1_w_down, w_in, w_branch_a, w_branch_b, w_out,
                  ffn2_w_gate, ffn2_w_up, ffn2_w_down, *, d_a, d_qk, d_b):
    bf = lambda w: w.astype(BF16)
    o = [0, d_a, 2 * d_a, 3 * d_a, 3 * d_a + d_qk, 3 * d_a + 2 * d_qk, 3 * d_a + 2 * d_qk + d_b,
         3 * d_a + 2 * d_qk + 2 * d_b]
    n_if = 2 * N_HEADS_B
    return dict(
        f1=(_pad_cols(bf(ffn1_w_gate), FF_TILE), _pad_cols(bf(ffn1_w_up), FF_TILE),
            _pad_rows(bf(ffn1_w_down), FF_TILE)),
        f2=(_pad_cols(bf(ffn2_w_gate), FF_TILE), _pad_cols(bf(ffn2_w_up), FF_TILE),
            _pad_rows(bf(ffn2_w_down), FF_TILE)),
        w_k=bf(w_in[:, o[1]:o[2]]), w_v=bf(w_in[:, o[2]:o[3]]),
        w_qkv=bf(jnp.concatenate([w_in[:, o[0]:o[1]], w_in[:, o[3]:o[6]]], axis=1)),
        w_gates=bf(jnp.concatenate([w_in[:, o[6]:o[7]], w_in[:, o[7] + n_if:]], axis=1)),
        w_if=_pad_cols(bf(w_in[:, o[7]:o[7] + n_if]), LANES),
        w_a=bf(w_branch_a), w_b=bf(w_branch_b), w_o=bf(w_out))


def _gate_rows(if_pre, batch, seq, pad_to=None):
    nh = N_HEADS_B
    g = if_pre[:, :2 * nh].reshape(batch, seq, 2, nh)
    if pad_to is not None and pad_to > seq:
        fill = jnp.broadcast_to(jnp.array([NEG_INF, -NEG_INF], F32)[None, None, :, None],
                                (batch, pad_to - seq, 2, nh))
        g = jnp.concatenate([g, fill], axis=1)
    g = g.transpose(0, 3, 2, 1)
    return jnp.pad(g, ((0, 0), (0, 0), (0, SUBLANES - 2), (0, 0)))


def _pad_seq(a, rows):
    return jnp.pad(a, ((0, 0), (0, rows - a.shape[1]), (0, 0)))


def _layer(x, w, norms, b_if, attend, mlstm_in, *, batch, seq, tm):
    ffn1_norm, mix_norm, mlstm_norm, ffn2_norm, final_norm = norms
    d = x.shape[1]
    nh = N_HEADS_B
    tn = 1024
    tn_wide = 2048
    d_a = w["w_k"].shape[1]
    x1, xn = _ffn(x, ffn1_norm, *w["f1"], mix_norm, tm=tm, final=False)
    k_a, v_a = _mm2(xn, w["w_k"], w["w_v"], tm=tm, tn=tn)
    qkv = _mm(xn, w["w_qkv"], tm=tm, tn=tn_wide, out_dtype=BF16, lead_cols=d_a,
              lead_scale=QK_LOG2_SCALE, name="proj_qkv")
    gates = _mm(xn, w["w_gates"], tm=tm, tn=tn_wide, out_dtype=BF16, act="sigmoid", name="proj_gates")
    if_pre = _mm(xn, w["w_if"], tm=tm, tn=LANES, out_dtype=F32, name="proj_if")

    y_a = attend(qkv, k_a, v_a)

    c0, n0, m0, chunk = mlstm_in
    dk, dv = c0.shape[2], c0.shape[3]
    cols = (d_a, d_a + nh * dk, d_a + 2 * nh * dk)
    gb = jnp.zeros((nh, SUBLANES, LANES), F32)
    gb = gb.at[:, 0, :].set(b_if[:nh, None]).at[:, 1, :].set(b_if[nh:, None])
    n0p = jnp.broadcast_to(n0[:, :, None, :], (batch, nh, SUBLANES, dk))
    m0p = jnp.broadcast_to(m0[:, :, None, None], (batch, nh, SUBLANES, LANES))
    if chunk > seq:
        qkv_m = _pad_seq(qkv.reshape(batch, seq, -1), chunk)
        og_m = _pad_seq(gates.reshape(batch, seq, -1), chunk)
        rows = _gate_rows(if_pre, batch, seq, pad_to=chunk)
        sp = chunk
    else:
        qkv_m, og_m = qkv.reshape(batch, seq, -1), gates.reshape(batch, seq, -1)
        rows = _gate_rows(if_pre, batch, seq)
        sp = seq
    y_b, c, n, m = _mlstm(qkv_m, og_m, cols, rows, gb, mlstm_norm.reshape(nh, 1, dv), c0, n0p, m0p,
                          batch=batch, seq=sp, chunk=chunk)
    y_b = y_b[:, :seq].reshape(batch * seq, nh * dv)

    mixed = _merge(y_a, y_b, w["w_a"], w["w_b"], gates, tm=tm, tn=tn)
    x2 = _mm_res(mixed, w["w_o"], x1, tm=tm, tn=tn)
    y = _ffn(x2, ffn2_norm, *w["f2"], final_norm, tm=tm, final=True)
    return y, k_a, v_a, (c, n[:, :, 0, :], m[:, :, 0, 0])


def kernel(x_prompt, x_sample, cache_k, cache_v, state_c, state_n, state_m, page_table, ffn1_norm, ffn1_w_gate, ffn1_w_up, ffn1_w_down, mix_norm, w_in, b_if, mlstm_norm, w_branch_a, w_branch_b, w_out, ffn2_norm, ffn2_w_gate, ffn2_w_up, ffn2_w_down, final_norm):
    depth = ffn1_norm.shape[0]
    assert depth == 1, "single-layer step"
    B, S, D = x_prompt.shape
    Bs, T, _ = x_sample.shape
    dh = HEAD_DIM_A
    nh_b = N_HEADS_B
    dk, dv = state_c.shape[3], state_c.shape[4]
    d_a = w_branch_a.shape[1]
    heads = d_a // dh
    n_pool, page = cache_k.shape[1], cache_k.shape[2]
    n_pages = page_table.shape[1]
    assert (n_pages * page) % MOBA_BLOCK == 0 and (n_pages * page) // MOBA_BLOCK >= MOBA_TOP_K
    l = 0

    w = _prep_weights(ffn1_w_gate[l], ffn1_w_up[l], ffn1_w_down[l], w_in[l], w_branch_a[l],
                      w_branch_b[l], w_out[l], ffn2_w_gate[l], ffn2_w_up[l], ffn2_w_down[l],
                      d_a=d_a, d_qk=nh_b * dk, d_b=nh_b * dv)
    norms = (ffn1_norm[l], mix_norm[l], mlstm_norm[l], ffn2_norm[l], final_norm)

    ck = cache_k.reshape(cache_k.shape[1:])
    cv = cache_v.reshape(cache_v.shape[1:])
    side = {}

    def attend_p(qkv, k_a, v_a):
        o, side["kmean"] = _moba_prompt(qkv.reshape(B, S, -1), 0, k_a.reshape(B, S, d_a),
                                        v_a.reshape(B, S, d_a), ck, page_table,
                                        batch=B, seq=S, heads=heads)
        return o.reshape(B * S, d_a)

    st0 = (jnp.zeros((B, nh_b, dk, dv), F32), jnp.zeros((B, nh_b, dk), F32),
           jnp.zeros((B, nh_b), F32), MLSTM_CHUNK_PROMPT)
    yp, kp, vp, (cp, n_p, mp) = _layer(x_prompt.reshape(B * S, D), w, norms, b_if[l], attend_p, st0,
                                       batch=B, seq=S, tm=512)

    def attend_s(qkv, k_a, v_a):
        q = qkv[:, :d_a].astype(F32).reshape(Bs, T, d_a)
        q_pad = _pad_seq(q, Q_PAD_ROWS)
        top = _topk(q_pad, side["kmean"], heads=heads)
        sel = top[:, :, :T, :MOBA_TOP_K].reshape(-1)
        o = _moba_sample(sel, page_table, q_pad, _pad_seq(k_a.reshape(Bs, T, d_a), Q_PAD_ROWS),
                         _pad_seq(v_a.reshape(Bs, T, d_a), Q_PAD_ROWS), ck, cv, heads=heads, tokens=T)
        return o[:, :T].reshape(Bs * T, d_a).astype(BF16)

    st_s = (state_c[l], state_n[l], state_m[l], MLSTM_CHUNK_SAMPLE)
    ys, k_s, v_s, (cs, n_s, ms) = _layer(x_sample.reshape(Bs * T, D), w, norms, b_if[l], attend_s, st_s,
                                         batch=Bs, seq=T, tm=Bs * T)

    return (yp.reshape(B, S, D), ys.reshape(Bs, T, D),
            kp.reshape(1, B, S, heads, dh), vp.reshape(1, B, S, heads, dh),
            cp[None], n_p[None], mp[None],
            k_s.reshape(1, Bs, T, heads, dh), v_s.reshape(1, Bs, T, heads, dh),
            cs[None], n_s[None], ms[None])
```

```python
import functools

import jax
import jax.numpy as jnp
from jax import lax
from jax.experimental import pallas as pl
from jax.experimental.pallas import tpu as pltpu

F32 = jnp.float32
BF16 = jnp.bfloat16
I32 = jnp.int32

HEAD_DIM_A = 128
MOBA_BLOCK = 256
MOBA_TOP_K = 3
N_HEADS_B = 4
EPS = 1e-6
NEG_INF = -1e30

LANES = 128
SUBLANES = 8
BF16_ROWS = 16
QK_LOG2_SCALE = HEAD_DIM_A ** -0.5 * 1.4426950408889634
VMEM_LIMIT_BYTES = 56 * 1024 * 1024
FF_TILE = 512
MLSTM_CHUNK_PROMPT = 256
MLSTM_CHUNK_SAMPLE = 128
SAMPLE_HEADS_PER_STEP = 4
PROJ_PAGES_PER_STEP = 8
Q_PAD_ROWS = BF16_ROWS


def _cparams(sem):
    return pltpu.CompilerParams(dimension_semantics=sem, vmem_limit_bytes=VMEM_LIMIT_BYTES)


def _rms(x, g):
    return x * lax.rsqrt(jnp.mean(x * x, axis=-1, keepdims=True) + EPS) * g


def _nt(a, b):
    return lax.dot_general(a, b, (((1,), (1,)), ((), ())), preferred_element_type=F32)


def _tn(a, b):
    return lax.dot_general(a, b, (((0,), (0,)), ((), ())), preferred_element_type=F32)


def _split3(x):
    hi = x.astype(BF16)
    r1 = x - hi.astype(F32)
    mid = r1.astype(BF16)
    lo = (r1 - mid.astype(F32)).astype(BF16)
    return hi, mid, lo


def _ffn_kernel(x_ref, g_ref, wg_ref, wu_ref, wd_ref, go_ref, *rest, final):
    if final:
        y_ref, xn_sc, acc_sc = rest
    else:
        y_ref, yn_ref, xn_sc, acc_sc = rest
    f = pl.program_id(1)

    @pl.when(f == 0)
    def _():
        xn_sc[...] = _rms(x_ref[...], g_ref[...]).astype(BF16)
        acc_sc[...] = jnp.zeros_like(acc_sc)

    xn = xn_sc[...]
    a = jnp.dot(xn, wg_ref[...], preferred_element_type=F32)
    u = jnp.dot(xn, wu_ref[...], preferred_element_type=F32)
    h = (a * jax.nn.sigmoid(a) * u).astype(BF16)
    acc_sc[...] += jnp.dot(h, wd_ref[...], preferred_element_type=F32)

    @pl.when(f == pl.num_programs(1) - 1)
    def _():
        y = x_ref[...] + 0.5 * acc_sc[...]
        if final:
            y_ref[...] = _rms(y, go_ref[...])
        else:
            y_ref[...] = y
            yn_ref[...] = _rms(y, go_ref[...]).astype(BF16)


def _ffn(x, g, wg, wu, wd, g_out, *, tm, final):
    m, d = x.shape
    ff = wg.shape[1]
    grid = (m // tm, ff // FF_TILE)
    row = pl.BlockSpec((tm, d), lambda i, f: (i, 0))
    vec = pl.BlockSpec((1, d), lambda i, f: (0, 0))
    in_specs = [row, vec,
                pl.BlockSpec((d, FF_TILE), lambda i, f: (0, f)),
                pl.BlockSpec((d, FF_TILE), lambda i, f: (0, f)),
                pl.BlockSpec((FF_TILE, d), lambda i, f: (f, 0)),
                vec]
    if final:
        out_shape = jax.ShapeDtypeStruct((m, d), F32)
        out_specs = row
    else:
        out_shape = (jax.ShapeDtypeStruct((m, d), F32), jax.ShapeDtypeStruct((m, d), BF16))
        out_specs = (row, row)
    return pl.pallas_call(
        functools.partial(_ffn_kernel, final=final),
        grid=grid, in_specs=in_specs, out_specs=out_specs, out_shape=out_shape,
        scratch_shapes=[pltpu.VMEM((tm, d), BF16), pltpu.VMEM((tm, d), F32)],
        compiler_params=_cparams(("parallel", "arbitrary")),
        name="ffn_final" if final else "ffn",
    )(x, g.reshape(1, d), wg, wu, wd, g_out.reshape(1, d))


class _PagedMeans:
    def __init__(self, cache_k, page_table, seq0, n_seqs, nsteps, grid_cols):
        n_pool, page, heads, dh = cache_k.shape
        n_pages = page_table.shape[1]
        ppb = MOBA_BLOCK // page
        assert (n_seqs * n_pages) % nsteps == 0
        pps = n_seqs * n_pages // nsteps
        assert pps % ppb == 0 and n_pages % pps == 0
        self.cache_k, self.page_table, self.seq0 = cache_k, page_table, seq0
        self.steps_per_seq = sps = n_pages // pps
        bps = pps // ppb
        self.out_spec = pl.BlockSpec(
            (1, bps, heads, dh),
            lambda a, b, pt: ((a * grid_cols + b) // sps, (a * grid_cols + b) % sps, 0, 0))
        self.out_shape = jax.ShapeDtypeStruct((n_seqs, n_pages // ppb, heads, dh), F32)
        self.scratch = [pltpu.VMEM((2, pps, page, heads, dh), F32), pltpu.SemaphoreType.DMA((2,))]


def _mm_kernel(*refs, act, lead_tiles, lead_scale, paged):
    if paged is None:
        x_ref, w_ref, o_ref = refs
    else:
        pt_ref, x_ref, w_ref, ck_ref, o_ref, km_ref, buf, sem = refs
        _page_block_means(pt_ref, ck_ref, km_ref, buf, sem, steps_per_seq=paged[0], seq0=paged[1])
    r = jnp.dot(x_ref[...], w_ref[...], preferred_element_type=F32)
    if act == "sigmoid":
        r = jax.nn.sigmoid(r)
    if lead_tiles:
        r = r * jnp.where(pl.program_id(0) < lead_tiles, lead_scale, 1.0)
    o_ref[...] = r.astype(o_ref.dtype)


def _mm(x, w, *, tm, tn, out_dtype, act=None, lead_cols=0, lead_scale=1.0, paged=None, name="proj"):
    m, k = x.shape
    n = w.shape[1]
    assert lead_cols % tn == 0
    grid = (n // tn, m // tm)
    in_specs = [pl.BlockSpec((tm, k), lambda j, i, *_: (i, 0)),
                pl.BlockSpec((k, tn), lambda j, i, *_: (0, j))]
    out_spec = pl.BlockSpec((tm, tn), lambda j, i, *_: (i, j))
    out_shape = jax.ShapeDtypeStruct((m, n), out_dtype)
    if paged is None:
        return pl.pallas_call(
            functools.partial(_mm_kernel, act=act, lead_tiles=lead_cols // tn, lead_scale=lead_scale,
                              paged=None),
            grid=grid, in_specs=in_specs, out_specs=out_spec, out_shape=out_shape,
            compiler_params=_cparams(("parallel", "arbitrary")),
            name=name,
        )(x, w)
    plan = _PagedMeans(*paged, nsteps=grid[0] * grid[1], grid_cols=grid[1])
    return pl.pallas_call(
        functools.partial(_mm_kernel, act=act, lead_tiles=lead_cols // tn, lead_scale=lead_scale,
                          paged=(plan.steps_per_seq, plan.seq0)),
        grid_spec=pltpu.PrefetchScalarGridSpec(
            num_scalar_prefetch=1, grid=grid,
            in_specs=in_specs + [pl.BlockSpec(memory_space=pl.ANY)],
            out_specs=(out_spec, plan.out_spec), scratch_shapes=plan.scratch),
        out_shape=(out_shape, plan.out_shape),
        compiler_params=_cparams(("arbitrary", "arbitrary")),
        name=name,
    )(plan.page_table, x, w, plan.cache_k)


def _mm2_kernel(x_ref, w1_ref, w2_ref, o1_ref, o2_ref):
    x = x_ref[...]
    o1_ref[...] = jnp.dot(x, w1_ref[...], preferred_element_type=F32)
    o2_ref[...] = jnp.dot(x, w2_ref[...], preferred_element_type=F32)


def _mm2(x, w1, w2, *, tm, tn):
    m, k = x.shape
    n = w1.shape[1]
    wspec = pl.BlockSpec((k, tn), lambda j, i: (0, j))
    ospec = pl.BlockSpec((tm, tn), lambda j, i: (i, j))
    return pl.pallas_call(
        _mm2_kernel,
        grid=(n // tn, m // tm),
        in_specs=[pl.BlockSpec((tm, k), lambda j, i: (i, 0)), wspec, wspec],
        out_specs=(ospec, ospec),
        out_shape=(jax.ShapeDtypeStruct((m, n), F32), jax.ShapeDtypeStruct((m, n), F32)),
        compiler_params=_cparams(("parallel", "arbitrary")),
        name="proj_kv",
    )(x, w1, w2)


def _merge_kernel(ya_ref, yb_ref, wa_ref, wb_ref, sga_ref, sgb_ref, o_ref):
    a = jnp.dot(ya_ref[...], wa_ref[...], preferred_element_type=F32)
    b = jnp.dot(yb_ref[...], wb_ref[...], preferred_element_type=F32)
    o_ref[...] = (sga_ref[...].astype(F32) * a + sgb_ref[...].astype(F32) * b).astype(o_ref.dtype)


def _merge(ya, yb, wa, wb, gates, *, tm, tn):
    m, k = ya.shape
    n = wa.shape[1]
    nj = n // tn
    xspec = pl.BlockSpec((tm, k), lambda j, i: (i, 0))
    wspec = pl.BlockSpec((k, tn), lambda j, i: (0, j))
    return pl.pallas_call(
        _merge_kernel,
        grid=(nj, m // tm),
        in_specs=[xspec, xspec, wspec, wspec,
                  pl.BlockSpec((tm, tn), lambda j, i: (i, nj + j)),
                  pl.BlockSpec((tm, tn), lambda j, i: (i, 2 * nj + j))],
        out_specs=pl.BlockSpec((tm, tn), lambda j, i: (i, j)),
        out_shape=jax.ShapeDtypeStruct((m, n), BF16),
        compiler_params=_cparams(("parallel", "arbitrary")),
        name="merge",
    )(ya, yb, wa, wb, gates, gates)


def _mm_res_kernel(x_ref, w_ref, r_ref, o_ref):
    o_ref[...] = r_ref[...] + jnp.dot(x_ref[...], w_ref[...], preferred_element_type=F32)


def _mm_res(x, w, res, *, tm, tn):
    m, k = x.shape
    n = w.shape[1]
    return pl.pallas_call(
        _mm_res_kernel,
        grid=(n // tn, m // tm),
        in_specs=[pl.BlockSpec((tm, k), lambda j, i: (i, 0)),
                  pl.BlockSpec((k, tn), lambda j, i: (0, j)),
                  pl.BlockSpec((tm, tn), lambda j, i: (i, j))],
        out_specs=pl.BlockSpec((tm, tn), lambda j, i: (i, j)),
        out_shape=jax.ShapeDtypeStruct((m, n), F32),
        compiler_params=_cparams(("parallel", "arbitrary")),
        name="out_proj",
    )(x, w, res)


def _page_block_means(pt_ref, ck_ref, km_ref, buf, sem, *, steps_per_seq, seq0):
    pps, page = buf.shape[1], buf.shape[2]
    ppb = MOBA_BLOCK // page
    step = pl.program_id(0) * pl.num_programs(1) + pl.program_id(1)
    nsteps = pl.num_programs(0) * pl.num_programs(1)

    def copies(step_, slot_):
        b_ = seq0 + step_ // steps_per_seq
        j_ = step_ % steps_per_seq
        return [pltpu.make_async_copy(ck_ref.at[pt_ref[b_, j_ * pps + i]], buf.at[slot_, i],
                                      sem.at[slot_]) for i in range(pps)]

    slot = step % 2

    @pl.when(step == 0)
    def _():
        for cp in copies(step, slot):
            cp.start()

    @pl.when(step + 1 < nsteps)
    def _():
        for cp in copies(step + 1, 1 - slot):
            cp.start()

    for cp in copies(step, slot):
        cp.wait()

    for i in range(pps // ppb):
        tot = functools.reduce(jnp.add, [jnp.sum(buf[slot, i * ppb + p], axis=0) for p in range(ppb)])
        km_ref[0, i] = tot * (1.0 / MOBA_BLOCK)


def _moba_prompt_kernel(pt_ref, q_ref, k_ref, v_ref, ck_ref, o_ref, km_ref, buf, sem, *,
                        seq, steps_per_seq, seq0):
    _page_block_means(pt_ref, ck_ref, km_ref, buf, sem, steps_per_seq=steps_per_seq, seq0=seq0)
    blk = MOBA_BLOCK
    nb = seq // blk
    dh = k_ref.shape[2]
    k32 = k_ref[0]
    kb = k32.astype(BF16)
    vt = jnp.concatenate([v_ref[0].T.astype(BF16), jnp.ones((BF16_ROWS, seq), BF16)], axis=0)
    kmean = jnp.concatenate(
        [jnp.mean(k32[n * blk:(n + 1) * blk], axis=0, keepdims=True) for n in range(nb)], axis=0)
    if nb % SUBLANES:
        kmean = jnp.concatenate(
            [kmean, jnp.zeros((SUBLANES - nb % SUBLANES, kmean.shape[1]), F32)], axis=0)
    km_hi = kmean.astype(BF16)
    km_lo = (kmean - km_hi.astype(F32)).astype(BF16)
    key_i = lax.broadcasted_iota(I32, (blk, blk), 0)
    qry_i = lax.broadcasted_iota(I32, (blk, blk), 1)
    causal = key_i <= qry_i

    for c in range(nb):
        q = q_ref[0, c * blk:(c + 1) * blk, :]
        blocks = []
        if c > 0:
            sel = None
            if c > MOBA_TOP_K:
                g = _nt(km_hi, q) + _nt(km_lo, q)
                rows = [g[n:n + 1, :] for n in range(c)]
                sel = []
                for n in range(c):
                    rank = jnp.zeros((1, blk), I32)
                    for m in range(c):
                        if m < n:
                            rank += (rows[m] >= rows[n]).astype(I32)
                        elif m > n:
                            rank += (rows[m] > rows[n]).astype(I32)
                    sel.append(rank < MOBA_TOP_K)
            for n in range(c):
                s = _nt(kb[n * blk:(n + 1) * blk], q)
                if sel is not None:
                    s = jnp.where(sel[n], s, NEG_INF)
                blocks.append(s)
        s_cur = _nt(kb[c * blk:(c + 1) * blk], q)
        blocks.append(jnp.where(causal, s_cur, NEG_INF))
        mx = functools.reduce(jnp.maximum, [jnp.max(b, axis=0, keepdims=True) for b in blocks])
        p_all = jnp.concatenate([jnp.exp2(b - mx).astype(BF16) for b in blocks], axis=0)
        ot = jnp.dot(vt[:, :(c + 1) * blk], p_all, preferred_element_type=F32)
        ot = ot[:dh] / ot[dh:dh + 1]
        o_ref[0, c * blk:(c + 1) * blk, :] = ot.T.astype(o_ref.dtype)


def _moba_prompt(q_src, q_col0, k, v, paged, *, batch, seq, heads):
    dh = HEAD_DIM_A
    assert seq % MOBA_BLOCK == 0
    plan = _PagedMeans(*paged, nsteps=batch * heads, grid_cols=heads)
    return pl.pallas_call(
        functools.partial(_moba_prompt_kernel, seq=seq, steps_per_seq=plan.steps_per_seq,
                          seq0=plan.seq0),
        grid_spec=pltpu.PrefetchScalarGridSpec(
            num_scalar_prefetch=1, grid=(batch, heads),
            in_specs=[pl.BlockSpec((1, seq, dh), lambda b, h, pt: (b, 0, q_col0 + h)),
                      pl.BlockSpec((1, seq, dh), lambda b, h, pt: (b, 0, h)),
                      pl.BlockSpec((1, seq, dh), lambda b, h, pt: (b, 0, h)),
                      pl.BlockSpec(memory_space=pl.ANY)],
            out_specs=(pl.BlockSpec((1, seq, dh), lambda b, h, pt: (b, 0, h)), plan.out_spec),
            scratch_shapes=plan.scratch),
        out_shape=(jax.ShapeDtypeStruct((batch, seq, heads * dh), BF16), plan.out_shape),
        compiler_params=_cparams(("arbitrary", "arbitrary")),
        name="moba_prompt",
    )(plan.page_table, q_src, k, v, plan.cache_k)


def _mlstm_kernel(q_ref, k_ref, v_ref, og_ref, gr_ref, gb_ref, nw_ref, c0_ref, n0_ref, m0_ref,
                  y_ref, c_ref, n_ref, m_ref, *, chunk, nchunks, dk):
    L = chunk
    r_i = lax.broadcasted_iota(I32, (L, L), 0)
    c_i = lax.broadcasted_iota(I32, (L, L), 1)
    causal = c_i <= r_i
    triu = (r_i <= c_i).astype(BF16)
    row_id = lax.broadcasted_iota(I32, (SUBLANES, L), 0)
    qscale = dk ** -0.5

    c_ref[0, 0] = c0_ref[0, 0]
    n_st = n0_ref[0, 0, 0:1, :]
    m_st = m0_ref[0, 0, 0:1, 0:1]
    nw = nw_ref[0]
    bias = gb_ref[0][:, 0:1]

    for t in range(nchunks):
        rows = slice(t * L, (t + 1) * L)
        g = gr_ref[0, 0, :, rows] + bias
        lf = jnp.minimum(g, 0.0) - jnp.log1p(jnp.exp(-jnp.abs(g)))
        g = jnp.where(row_id == 1, lf, g)
        gh, gm, gl = _split3(g)
        cum = (jnp.dot(gh, triu, preferred_element_type=F32)
               + jnp.dot(gm, triu, preferred_element_type=F32)
               + jnp.dot(gl, triu, preferred_element_type=F32))
        both = jnp.concatenate(
            [g, cum, jnp.zeros((LANES - 2 * SUBLANES, L), F32)], axis=0)
        cols = both.T
        i_col = cols[:, 0:1]
        b_col = cols[:, SUBLANES + 1:SUBLANES + 2]
        i_row = g[0:1, :]
        b_row = cum[1:2, :]

        d = jnp.where(causal, b_col - b_row + i_row, NEG_INF)
        inter = b_col + m_st
        m_row = jnp.maximum(inter, jnp.max(d, axis=1, keepdims=True))
        w_intra = jnp.exp(d - m_row)
        w_inter = jnp.exp(inter - m_row)

        qc = q_ref[0, rows, :] * qscale
        kc = k_ref[0, rows, :]
        vc = v_ref[0, rows, :]
        c_st = c_ref[0, 0]
        s = _nt(qc, kc) * w_intra
        num = (jnp.dot(s.astype(BF16), vc, preferred_element_type=F32)
               + w_inter * jnp.dot(qc, c_st.astype(BF16), preferred_element_type=F32))
        qn = jnp.sum(qc.astype(F32) * n_st, axis=1, keepdims=True)
        den = jnp.sum(s, axis=1, keepdims=True) + w_inter * qn
        h = num / jnp.maximum(jnp.abs(den), jnp.exp(-m_row))
        hn = _rms(h, nw)
        y_ref[0, rows, :] = (og_ref[0, rows, :].astype(F32) * hn).astype(y_ref.dtype)

        m_new = m_row[L - 1:L, :]
        b_last = b_col[L - 1:L, :]
        decay = jnp.exp(b_last + m_st - m_new)
        w_col = jnp.exp(b_last - b_col + i_col - m_new)
        c_ref[0, 0] = decay * c_st + _tn(kc, (vc.astype(F32) * w_col).astype(BF16))
        n_st = decay * n_st + jnp.sum(kc.astype(F32) * w_col, axis=0, keepdims=True)
        m_st = m_new

    n_ref[0, 0] = jnp.broadcast_to(n_st, n_ref.shape[2:])
    m_ref[0, 0] = jnp.broadcast_to(m_st, m_ref.shape[2:])


def _mlstm(qkv_src, og_src, cols, gate_rows, gate_bias, norm_w, c0, n0, m0, *, batch, seq, chunk):
    nh = N_HEADS_B
    dk, dv = c0.shape[2], c0.shape[3]
    q0, k0, v0 = cols[0] // dk, cols[1] // dk, cols[2] // dv
    bh = lambda b, h: (b, h, 0, 0)
    return pl.pallas_call(
        functools.partial(_mlstm_kernel, chunk=chunk, nchunks=seq // chunk, dk=dk),
        grid=(batch, nh),
        in_specs=[pl.BlockSpec((1, seq, dk), lambda b, h: (b, 0, q0 + h)),
                  pl.BlockSpec((1, seq, dk), lambda b, h: (b, 0, k0 + h)),
                  pl.BlockSpec((1, seq, dv), lambda b, h: (b, 0, v0 + h)),
                  pl.BlockSpec((1, seq, dv), lambda b, h: (b, 0, h)),
                  pl.BlockSpec((1, 1, SUBLANES, seq), bh),
                  pl.BlockSpec((1, SUBLANES, LANES), lambda b, h: (h, 0, 0)),
                  pl.BlockSpec((1, 1, dv), lambda b, h: (h, 0, 0)),
                  pl.BlockSpec((1, 1, dk, dv), bh),
                  pl.BlockSpec((1, 1, SUBLANES, dk), bh),
                  pl.BlockSpec((1, 1, SUBLANES, LANES), bh)],
        out_specs=(pl.BlockSpec((1, seq, dv), lambda b, h: (b, 0, h)),
                   pl.BlockSpec((1, 1, dk, dv), bh),
                   pl.BlockSpec((1, 1, SUBLANES, dk), bh),
                   pl.BlockSpec((1, 1, SUBLANES, LANES), bh)),
        out_shape=(jax.ShapeDtypeStruct((batch, seq, nh * dv), BF16),
                   jax.ShapeDtypeStruct((batch, nh, dk, dv), F32),
                   jax.ShapeDtypeStruct((batch, nh, SUBLANES, dk), F32),
                   jax.ShapeDtypeStruct((batch, nh, SUBLANES, LANES), F32)),
        compiler_params=_cparams(("parallel", "parallel")),
        name="mlstm",
    )(qkv_src, qkv_src, qkv_src, og_src, gate_rows, gate_bias, norm_w, c0, n0, m0)


def _topk_kernel(q_ref, km_ref, o_ref, *, heads):
    dh = HEAD_DIM_A
    nblk = km_ref.shape[1]
    lane = lax.broadcasted_iota(I32, (q_ref.shape[1], nblk), 1)
    lane_f = lane.astype(F32)
    out_lane = lax.broadcasted_iota(I32, (q_ref.shape[1], LANES), 1)
    for h in range(heads):
        q3 = _split3(q_ref[0, :, h * dh:(h + 1) * dh])
        k3 = _split3(km_ref[0, :, h, :])
        gate = functools.reduce(jnp.add, [_nt(q3[i], k3[j]) for i in range(3) for j in range(3 - i)])
        out = jnp.zeros((q_ref.shape[1], LANES), I32)
        for k in range(MOBA_TOP_K):
            mx = jnp.max(gate, axis=1, keepdims=True)
            idx = jnp.min(jnp.where(gate == mx, lane_f, float(nblk)), axis=1, keepdims=True).astype(I32)
            out = jnp.where(out_lane == k, idx, out)
            gate = jnp.where(lane == idx, -jnp.inf, gate)
        o_ref[0, h] = out


def _topk(q_pad, kmean, *, heads):
    batch, rows, width = q_pad.shape
    nblk = kmean.shape[1]
    return pl.pallas_call(
        functools.partial(_topk_kernel, heads=heads),
        grid=(batch,),
        in_specs=[pl.BlockSpec((1, rows, width), lambda b: (b, 0, 0)),
                  pl.BlockSpec((1, nblk, heads, HEAD_DIM_A), lambda b: (b, 0, 0, 0))],
        out_specs=pl.BlockSpec((1, heads, rows, LANES), lambda b: (b, 0, 0, 0)),
        out_shape=jax.ShapeDtypeStruct((batch, heads, rows, LANES), I32),
        compiler_params=_cparams(("parallel",)),
        name="moba_topk",
    )(q_pad, kmean)


def _moba_sample_kernel(sel_ref, pt_ref, q_ref, kn_ref, vn_ref, ck_ref, cv_ref, o_ref,
                        kbuf, vbuf, sem, *, heads, tokens):
    hps = SAMPLE_HEADS_PER_STEP
    groups = heads // hps
    step = pl.program_id(0) * groups + pl.program_id(1)
    nsteps = pl.num_programs(0) * groups
    page, dh = kbuf.shape[2], kbuf.shape[3]
    ppb = MOBA_BLOCK // page
    nsel = tokens * MOBA_TOP_K
    per_head = nsel * ppb

    def copies(step_, slot_):
        b_ = step_ // groups
        out = []
        for hh in range(hps):
            h_ = (step_ % groups) * hps + hh
            for j in range(nsel):
                blk = sel_ref[(b_ * heads + h_) * nsel + j]
                for p in range(ppb):
                    phys = pt_ref[b_, blk * ppb + p]
                    dst = hh * per_head + j * ppb + p
                    out.append(pltpu.make_async_copy(
                        ck_ref.at[phys, :, h_, :], kbuf.at[slot_, dst], sem.at[0, slot_]))
                    out.append(pltpu.make_async_copy(
                        cv_ref.at[phys, :, h_, :], vbuf.at[slot_, dst], sem.at[1, slot_]))
        return out

    slot = step % 2

    @pl.when(step == 0)
    def _():
        for cp in copies(step, slot):
            cp.start()

    @pl.when(step + 1 < nsteps)
    def _():
        for cp in copies(step + 1, 1 - slot):
            cp.start()

    for cp in copies(step, slot):
        cp.wait()

    npairs = hps * tokens
    per_tok = MOBA_TOP_K * ppb
    q = q_ref[0, 0]
    qb = q.astype(BF16)
    rows_n = q.shape[0]
    row_id = lax.broadcasted_iota(I32, (rows_n, 1), 0)
    s_sel = jnp.zeros((rows_n, per_tok * page), F32)
    for p in range(npairs):
        kp = kbuf[slot, p * per_tok:(p + 1) * per_tok].reshape(per_tok * page, dh).astype(BF16)
        s_sel = jnp.where(row_id == p, _nt(qb, kp), s_sel)
    pr = lax.broadcasted_iota(I32, (rows_n, rows_n), 0)
    pc = lax.broadcasted_iota(I32, (rows_n, rows_n), 1)
    ok = functools.reduce(jnp.logical_or, [
        (pr >= hh * tokens) & (pr < (hh + 1) * tokens) & (pc >= hh * tokens) & (pc <= pr)
        for hh in range(hps)])
    s_new = jnp.where(ok, _nt(qb, kn_ref[0, 0].astype(BF16)), NEG_INF)
    mx = jnp.maximum(jnp.max(s_sel, axis=1, keepdims=True), jnp.max(s_new, axis=1, keepdims=True))
    p_sel = jnp.exp2(s_sel - mx)
    p_new = jnp.exp2(s_new - mx)
    den = jnp.sum(p_sel, axis=1, keepdims=True) + jnp.sum(p_new, axis=1, keepdims=True)
    pb = p_sel.astype(BF16)
    o = jnp.zeros((rows_n, dh), F32)
    for p in range(npairs):
        vp = vbuf[slot, p * per_tok:(p + 1) * per_tok].reshape(per_tok * page, dh).astype(BF16)
        o = jnp.where(row_id == p, jnp.dot(pb, vp, preferred_element_type=F32), o)
    vn = vn_ref[0, 0]
    for c in range(npairs):
        o = o + p_new[:, c:c + 1] * vn[c:c + 1, :]
    o_ref[0, 0] = o / den


def _pair_rows(a, rows):
    b, t, width = a.shape
    hps, dh = SAMPLE_HEADS_PER_STEP, HEAD_DIM_A
    a = a.reshape(b, t, width // (hps * dh), hps, dh).transpose(0, 2, 3, 1, 4)
    a = a.reshape(b, width // (hps * dh), hps * t, dh)
    return jnp.pad(a, ((0, 0), (0, 0), (0, rows - hps * t), (0, 0)))


def _moba_sample(sel, page_table, q, k_new, v_new, cache_k, cache_v, *, heads):
    batch, tokens, width = q.shape
    dh = HEAD_DIM_A
    n_pool, page = cache_k.shape[0], cache_k.shape[1]
    ppb = MOBA_BLOCK // page
    hps = SAMPLE_HEADS_PER_STEP
    assert heads % hps == 0
    groups = heads // hps
    rows = -(-hps * tokens // BF16_ROWS) * BF16_ROWS
    nbuf = hps * tokens * MOBA_TOP_K * ppb
    blk = pl.BlockSpec((1, 1, rows, dh), lambda b, g, s, p: (b, g, 0, 0))
    o = pl.pallas_call(
        functools.partial(_moba_sample_kernel, heads=heads, tokens=tokens),
        grid_spec=pltpu.PrefetchScalarGridSpec(
            num_scalar_prefetch=2, grid=(batch, groups),
            in_specs=[blk, blk, blk,
                      pl.BlockSpec(memory_space=pl.ANY), pl.BlockSpec(memory_space=pl.ANY)],
            out_specs=blk,
            scratch_shapes=[pltpu.VMEM((2, nbuf, page, dh), F32),
                            pltpu.VMEM((2, nbuf, page, dh), F32),
                            pltpu.SemaphoreType.DMA((2, 2))]),
        out_shape=jax.ShapeDtypeStruct((batch, groups, rows, dh), F32),
        compiler_params=_cparams(("arbitrary", "arbitrary")),
        name="moba_sample",
    )(sel, page_table, _pair_rows(q, rows), _pair_rows(k_new, rows), _pair_rows(v_new, rows),
      cache_k, cache_v)
    o = o[:, :, :hps * tokens].reshape(batch, groups, hps, tokens, dh)
    return o.transpose(0, 3, 1, 2, 4).reshape(batch, tokens, width)


def _pad_cols(w, mult):
    pad = (-w.shape[1]) % mult
    return jnp.pad(w, ((0, 0), (0, pad))) if pad else w


def _pad_rows(w, mult):
    pad = (-w.shape[0]) % mult
    return jnp.pad(w, ((0, pad), (0, 0))) if pad else w


def _prep_weights(ffn1_w_gate, ffn1_w_up, ffn1_w_down, w_in, w_branch_a, w_branch_b, w_out,
                  ffn2_w_gate, ffn2_w_up, ffn2_w_down, *, d_a, d_qk, d_b):
    bf = lambda w: w.astype(BF16)
    o = [0, d_a, 2 * d_a, 3 * d_a, 3 * d_a + d_qk, 3 * d_a + 2 * d_qk, 3 * d_a + 2 * d_qk + d_b,
         3 * d_a + 2 * d_qk + 2 * d_b]
    n_if = 2 * N_HEADS_B
    return dict(
        f1=(_pad_cols(bf(ffn1_w_gate), FF_TILE), _pad_cols(bf(ffn1_w_up), FF_TILE),
            _pad_rows(bf(ffn1_w_down), FF_TILE)),
        f2=(_pad_cols(bf(ffn2_w_gate), FF_TILE), _pad_cols(bf(ffn2_w_up), FF_TILE),
            _pad_rows(bf(ffn2_w_down), FF_TILE)),
        w_k=bf(w_in[:, o[1]:o[2]]), w_v=bf(w_in[:, o[2]:o[3]]),
        w_qkv=bf(jnp.concatenate([w_in[:, o[0]:o[1]], w_in[:, o[3]:o[6]]], axis=1)),
        w_gates=bf(jnp.concatenate([w_in[:, o[6]:o[7]], w_in[:, o[7] + n_if:]], axis=1)),
        w_if=_pad_cols(bf(w_in[:, o[7]:o[7] + n_if]), LANES),
        w_a=bf(w_branch_a), w_b=bf(w_branch_b), w_o=bf(w_out))


def _gate_rows(if_pre, batch, seq, pad_to=None):
    nh = N_HEADS_B
    g = if_pre[:, :2 * nh].reshape(batch, seq, 2, nh)
    if pad_to is not None and pad_to > seq:
        fill = jnp.broadcast_to(jnp.array([NEG_INF, -NEG_INF], F32)[None, None, :, None],
                                (batch, pad_to - seq, 2, nh))
        g = jnp.concatenate([g, fill], axis=1)
    g = g.transpose(0, 3, 2, 1)
    return jnp.pad(g, ((0, 0), (0, 0), (0, SUBLANES - 2), (0, 0)))


def _pad_seq(a, rows):
    return jnp.pad(a, ((0, 0), (0, rows - a.shape[1]), (0, 0)))


def _layer(x, w, norms, b_if, attend, mlstm_in, *, batch, seq, tm, pool=None):
    ffn1_norm, mix_norm, mlstm_norm, ffn2_norm, final_norm = norms
    d = x.shape[1]
    nh = N_HEADS_B
    tn = 1024
    tn_wide = 2048
    d_a = w["w_k"].shape[1]
    x1, xn = _ffn(x, ffn1_norm, *w["f1"], mix_norm, tm=tm, final=False)
    k_a, v_a = _mm2(xn, w["w_k"], w["w_v"], tm=tm, tn=tn)
    qkv_args = dict(tm=tm, tn=tn_wide, out_dtype=BF16, lead_cols=d_a, lead_scale=QK_LOG2_SCALE,
                    name="proj_qkv")
    gate_args = dict(tm=tm, tn=tn_wide, out_dtype=BF16, act="sigmoid", name="proj_gates")
    if pool is None:
        qkv = _mm(xn, w["w_qkv"], **qkv_args)
        gates = _mm(xn, w["w_gates"], **gate_args)
        y_a = attend(qkv, k_a, v_a)
        kmean = None
    else:
        cache_k, page_table = pool
        n_seqs, n_pages = page_table.shape
        proj_steps = (w["w_qkv"].shape[1] // tn_wide) * (x.shape[0] // tm)
        assert (proj_steps * PROJ_PAGES_PER_STEP) % n_pages == 0
        s1 = proj_steps * PROJ_PAGES_PER_STEP // n_pages
        assert 2 * s1 < n_seqs
        qkv, km1 = _mm(xn, w["w_qkv"], paged=(cache_k, page_table, 0, s1), **qkv_args)
        gates, km2 = _mm(xn, w["w_gates"], paged=(cache_k, page_table, s1, s1), **gate_args)
        y_a, km3 = attend(qkv, k_a, v_a, (cache_k, page_table, 2 * s1, n_seqs - 2 * s1))
        kmean = jnp.concatenate([km1, km2, km3], axis=0)
    if_pre = _mm(xn, w["w_if"], tm=tm, tn=LANES, out_dtype=F32, name="proj_if")

    c0, n0, m0, chunk = mlstm_in
    dk, dv = c0.shape[2], c0.shape[3]
    cols = (d_a, d_a + nh * dk, d_a + 2 * nh * dk)
    gb = jnp.zeros((nh, SUBLANES, LANES), F32)
    gb = gb.at[:, 0, :].set(b_if[:nh, None]).at[:, 1, :].set(b_if[nh:, None])
    n0p = jnp.broadcast_to(n0[:, :, None, :], (batch, nh, SUBLANES, dk))
    m0p = jnp.broadcast_to(m0[:, :, None, None], (batch, nh, SUBLANES, LANES))
    if chunk > seq:
        qkv_m = _pad_seq(qkv.reshape(batch, seq, -1), chunk)
        og_m = _pad_seq(gates.reshape(batch, seq, -1), chunk)
        rows = _gate_rows(if_pre, batch, seq, pad_to=chunk)
        sp = chunk
    else:
        qkv_m, og_m = qkv.reshape(batch, seq, -1), gates.reshape(batch, seq, -1)
        rows = _gate_rows(if_pre, batch, seq)
        sp = seq
    y_b, c, n, m = _mlstm(qkv_m, og_m, cols, rows, gb, mlstm_norm.reshape(nh, 1, dv), c0, n0p, m0p,
                          batch=batch, seq=sp, chunk=chunk)
    y_b = y_b[:, :seq].reshape(batch * seq, nh * dv)

    mixed = _merge(y_a, y_b, w["w_a"], w["w_b"], gates, tm=tm, tn=tn)
    x2 = _mm_res(mixed, w["w_o"], x1, tm=tm, tn=tn)
    y = _ffn(x2, ffn2_norm, *w["f2"], final_norm, tm=tm, final=True)
    return y, k_a, v_a, (c, n[:, :, 0, :], m[:, :, 0, 0]), kmean


def kernel(x_prompt, x_sample, cache_k, cache_v, state_c, state_n, state_m, page_table, ffn1_norm, ffn1_w_gate, ffn1_w_up, ffn1_w_down, mix_norm, w_in, b_if, mlstm_norm, w_branch_a, w_branch_b, w_out, ffn2_norm, ffn2_w_gate, ffn2_w_up, ffn2_w_down, final_norm):
    depth = ffn1_norm.shape[0]
    assert depth == 1, "single-layer step"
    B, S, D = x_prompt.shape
    Bs, T, _ = x_sample.shape
    dh = HEAD_DIM_A
    nh_b = N_HEADS_B
    dk, dv = state_c.shape[3], state_c.shape[4]
    d_a = w_branch_a.shape[1]
    heads = d_a // dh
    n_pool, page = cache_k.shape[1], cache_k.shape[2]
    n_pages = page_table.shape[1]
    assert (n_pages * page) % MOBA_BLOCK == 0 and (n_pages * page) // MOBA_BLOCK >= MOBA_TOP_K
    l = 0

    w = _prep_weights(ffn1_w_gate[l], ffn1_w_up[l], ffn1_w_down[l], w_in[l], w_branch_a[l],
                      w_branch_b[l], w_out[l], ffn2_w_gate[l], ffn2_w_up[l], ffn2_w_down[l],
                      d_a=d_a, d_qk=nh_b * dk, d_b=nh_b * dv)
    norms = (ffn1_norm[l], mix_norm[l], mlstm_norm[l], ffn2_norm[l], final_norm)

    ck = cache_k.reshape(cache_k.shape[1:])
    cv = cache_v.reshape(cache_v.shape[1:])

    def attend_p(qkv, k_a, v_a, paged):
        o, km = _moba_prompt(qkv.reshape(B, S, -1), 0, k_a.reshape(B, S, d_a), v_a.reshape(B, S, d_a),
                             paged, batch=B, seq=S, heads=heads)
        return o.reshape(B * S, d_a), km

    st0 = (jnp.zeros((B, nh_b, dk, dv), F32), jnp.zeros((B, nh_b, dk), F32),
           jnp.zeros((B, nh_b), F32), MLSTM_CHUNK_PROMPT)
    yp, kp, vp, (cp, n_p, mp), kmean = _layer(x_prompt.reshape(B * S, D), w, norms, b_if[l], attend_p,
                                              st0, batch=B, seq=S, tm=512, pool=(ck, page_table))

    def attend_s(qkv, k_a, v_a):
        q = qkv[:, :d_a].astype(F32).reshape(Bs, T, d_a)
        top = _topk(_pad_seq(q, Q_PAD_ROWS), kmean, heads=heads)
        sel = top[:, :, :T, :MOBA_TOP_K].reshape(-1)
        o = _moba_sample(sel, page_table, q, k_a.reshape(Bs, T, d_a), v_a.reshape(Bs, T, d_a), ck, cv,
                         heads=heads)
        return o.reshape(Bs * T, d_a).astype(BF16)

    st_s = (state_c[l], state_n[l], state_m[l], MLSTM_CHUNK_SAMPLE)
    ys, k_s, v_s, (cs, n_s, ms), _ = _layer(x_sample.reshape(Bs * T, D), w, norms, b_if[l], attend_s,
                                            st_s, batch=Bs, seq=T, tm=Bs * T)

    return (yp.reshape(B, S, D), ys.reshape(Bs, T, D),
            kp.reshape(1, B, S, heads, dh), vp.reshape(1, B, S, heads, dh),
            cp[None], n_p[None], mp[None],
            k_s.reshape(1, Bs, T, heads, dh), v_s.reshape(1, Bs, T, heads, dh),
            cs[None], n_s[None], ms[None])
```

```python
import functools

import jax
import jax.numpy as jnp
from jax import lax
from jax.experimental import pallas as pl
from jax.experimental.pallas import tpu as pltpu

F32 = jnp.float32
BF16 = jnp.bfloat16
I32 = jnp.int32

HEAD_DIM_A = 128
MOBA_BLOCK = 256
MOBA_TOP_K = 3
N_HEADS_B = 4
EPS = 1e-6
NEG_INF = -1e30

LANES = 128
SUBLANES = 8
BF16_ROWS = 16
QK_LOG2_SCALE = HEAD_DIM_A ** -0.5 * 1.4426950408889634
VMEM_LIMIT_BYTES = 56 * 1024 * 1024
FF_TILE = 512
MLSTM_CHUNK_PROMPT = 256
MLSTM_CHUNK_SAMPLE = 128
SAMPLE_HEADS_PER_STEP = 4
PROJ_PAGES_PER_STEP = 8
Q_PAD_ROWS = BF16_ROWS


def _cparams(sem):
    return pltpu.CompilerParams(dimension_semantics=sem, vmem_limit_bytes=VMEM_LIMIT_BYTES)


def _rms(x, g):
    return x * lax.rsqrt(jnp.mean(x * x, axis=-1, keepdims=True) + EPS) * g


def _nt(a, b):
    return lax.dot_general(a, b, (((1,), (1,)), ((), ())), preferred_element_type=F32)


def _tn(a, b):
    return lax.dot_general(a, b, (((0,), (0,)), ((), ())), preferred_element_type=F32)


def _split3(x):
    hi = x.astype(BF16)
    r1 = x - hi.astype(F32)
    mid = r1.astype(BF16)
    lo = (r1 - mid.astype(F32)).astype(BF16)
    return hi, mid, lo


def _ffn_kernel(x_ref, g_ref, wg_ref, wu_ref, wd_ref, go_ref, *rest, final):
    if final:
        y_ref, xn_sc, acc_sc = rest
    else:
        y_ref, yn_ref, xn_sc, acc_sc = rest
    f = pl.program_id(1)

    @pl.when(f == 0)
    def _():
        xn_sc[...] = _rms(x_ref[...], g_ref[...]).astype(BF16)
        acc_sc[...] = jnp.zeros_like(acc_sc)

    xn = xn_sc[...]
    a = jnp.dot(xn, wg_ref[...], preferred_element_type=F32)
    u = jnp.dot(xn, wu_ref[...], preferred_element_type=F32)
    h = (a * jax.nn.sigmoid(a) * u).astype(BF16)
    acc_sc[...] += jnp.dot(h, wd_ref[...], preferred_element_type=F32)

    @pl.when(f == pl.num_programs(1) - 1)
    def _():
        y = x_ref[...] + 0.5 * acc_sc[...]
        if final:
            y_ref[...] = _rms(y, go_ref[...])
        else:
            y_ref[...] = y
            yn_ref[...] = _rms(y, go_ref[...]).astype(BF16)


def _ffn(x, g, wg, wu, wd, g_out, *, tm, final):
    m, d = x.shape
    ff = wg.shape[1]
    grid = (m // tm, ff // FF_TILE)
    row = pl.BlockSpec((tm, d), lambda i, f: (i, 0))
    vec = pl.BlockSpec((1, d), lambda i, f: (0, 0))
    in_specs = [row, vec,
                pl.BlockSpec((d, FF_TILE), lambda i, f: (0, f)),
                pl.BlockSpec((d, FF_TILE), lambda i, f: (0, f)),
                pl.BlockSpec((FF_TILE, d), lambda i, f: (f, 0)),
                vec]
    if final:
        out_shape = jax.ShapeDtypeStruct((m, d), F32)
        out_specs = row
    else:
        out_shape = (jax.ShapeDtypeStruct((m, d), F32), jax.ShapeDtypeStruct((m, d), BF16))
        out_specs = (row, row)
    return pl.pallas_call(
        functools.partial(_ffn_kernel, final=final),
        grid=grid, in_specs=in_specs, out_specs=out_specs, out_shape=out_shape,
        scratch_shapes=[pltpu.VMEM((tm, d), BF16), pltpu.VMEM((tm, d), F32)],
        compiler_params=_cparams(("parallel", "arbitrary")),
        name="ffn_final" if final else "ffn",
    )(x, g.reshape(1, d), wg, wu, wd, g_out.reshape(1, d))


class _PagedMeans:
    def __init__(self, cache_k, page_table, seq0, n_seqs, nsteps, grid_cols):
        n_pool, page, heads, dh = cache_k.shape
        n_pages = page_table.shape[1]
        ppb = MOBA_BLOCK // page
        assert (n_seqs * n_pages) % nsteps == 0
        pps = n_seqs * n_pages // nsteps
        assert pps % ppb == 0 and n_pages % pps == 0
        self.cache_k, self.page_table, self.seq0 = cache_k, page_table, seq0
        self.steps_per_seq = sps = n_pages // pps
        bps = pps // ppb
        self.out_spec = pl.BlockSpec(
            (1, bps, heads, dh),
            lambda a, b, pt: ((a * grid_cols + b) // sps, (a * grid_cols + b) % sps, 0, 0))
        self.out_shape = jax.ShapeDtypeStruct((n_seqs, n_pages // ppb, heads, dh), F32)
        self.scratch = [pltpu.VMEM((2, pps, page, heads, dh), F32), pltpu.SemaphoreType.DMA((2,))]


def _mm_kernel(*refs, act, lead_tiles, lead_scale, paged):
    if paged is None:
        x_ref, w_ref, o_ref = refs
    else:
        pt_ref, x_ref, w_ref, ck_ref, o_ref, km_ref, buf, sem = refs
        _page_block_means(pt_ref, ck_ref, km_ref, buf, sem, steps_per_seq=paged[0], seq0=paged[1])
    r = jnp.dot(x_ref[...], w_ref[...], preferred_element_type=F32)
    if act == "sigmoid":
        r = jax.nn.sigmoid(r)
    if lead_tiles:
        r = r * jnp.where(pl.program_id(0) < lead_tiles, lead_scale, 1.0)
    o_ref[...] = r.astype(o_ref.dtype)


def _mm(x, w, *, tm, tn, out_dtype, act=None, lead_cols=0, lead_scale=1.0, paged=None, name="proj"):
    m, k = x.shape
    n = w.shape[1]
    assert lead_cols % tn == 0
    grid = (n // tn, m // tm)
    in_specs = [pl.BlockSpec((tm, k), lambda j, i, *_: (i, 0)),
                pl.BlockSpec((k, tn), lambda j, i, *_: (0, j))]
    out_spec = pl.BlockSpec((tm, tn), lambda j, i, *_: (i, j))
    out_shape = jax.ShapeDtypeStruct((m, n), out_dtype)
    if paged is None:
        return pl.pallas_call(
            functools.partial(_mm_kernel, act=act, lead_tiles=lead_cols // tn, lead_scale=lead_scale,
                              paged=None),
            grid=grid, in_specs=in_specs, out_specs=out_spec, out_shape=out_shape,
            compiler_params=_cparams(("parallel", "arbitrary")),
            name=name,
        )(x, w)
    plan = _PagedMeans(*paged, nsteps=grid[0] * grid[1], grid_cols=grid[1])
    return pl.pallas_call(
        functools.partial(_mm_kernel, act=act, lead_tiles=lead_cols // tn, lead_scale=lead_scale,
                          paged=(plan.steps_per_seq, plan.seq0)),
        grid_spec=pltpu.PrefetchScalarGridSpec(
            num_scalar_prefetch=1, grid=grid,
            in_specs=in_specs + [pl.BlockSpec(memory_space=pl.ANY)],
            out_specs=(out_spec, plan.out_spec), scratch_shapes=plan.scratch),
        out_shape=(out_shape, plan.out_shape),
        compiler_params=_cparams(("arbitrary", "arbitrary")),
        name=name,
    )(plan.page_table, x, w, plan.cache_k)


def _mm2_kernel(x_ref, w1_ref, w2_ref, wt_ref, o1_ref, o2_ref, ot_ref):
    x = x_ref[...]
    o1_ref[...] = jnp.dot(x, w1_ref[...], preferred_element_type=F32)
    o2_ref[...] = jnp.dot(x, w2_ref[...], preferred_element_type=F32)
    ot_ref[0] = _nt(wt_ref[...], x)


def _mm2(x, w1, w2, wt, *, tm, tn):
    m, k = x.shape
    n = w1.shape[1]
    r = wt.shape[0]
    nj = n // tn
    wspec = pl.BlockSpec((k, tn), lambda j, i: (0, j))
    ospec = pl.BlockSpec((tm, tn), lambda j, i: (i, j))
    o1, o2, ot = pl.pallas_call(
        _mm2_kernel,
        grid=(nj, m // tm),
        in_specs=[pl.BlockSpec((tm, k), lambda j, i: (i, 0)), wspec, wspec,
                  pl.BlockSpec((r, k), lambda j, i: (0, 0))],
        out_specs=(ospec, ospec, pl.BlockSpec((1, r, tm), lambda j, i: (j, 0, i))),
        out_shape=(jax.ShapeDtypeStruct((m, n), F32), jax.ShapeDtypeStruct((m, n), F32),
                   jax.ShapeDtypeStruct((nj, r, m), F32)),
        compiler_params=_cparams(("parallel", "arbitrary")),
        name="proj_kv",
    )(x, w1, w2, wt)
    return o1, o2, ot[0]


def _merge_kernel(ya_ref, yb_ref, wa_ref, wb_ref, sga_ref, sgb_ref, o_ref):
    a = jnp.dot(ya_ref[...], wa_ref[...], preferred_element_type=F32)
    b = jnp.dot(yb_ref[...], wb_ref[...], preferred_element_type=F32)
    o_ref[...] = (sga_ref[...].astype(F32) * a + sgb_ref[...].astype(F32) * b).astype(o_ref.dtype)


def _merge(ya, yb, wa, wb, gates, *, tm, tn):
    m, k = ya.shape
    n = wa.shape[1]
    nj = n // tn
    xspec = pl.BlockSpec((tm, k), lambda j, i: (i, 0))
    wspec = pl.BlockSpec((k, tn), lambda j, i: (0, j))
    return pl.pallas_call(
        _merge_kernel,
        grid=(nj, m // tm),
        in_specs=[xspec, xspec, wspec, wspec,
                  pl.BlockSpec((tm, tn), lambda j, i: (i, nj + j)),
                  pl.BlockSpec((tm, tn), lambda j, i: (i, 2 * nj + j))],
        out_specs=pl.BlockSpec((tm, tn), lambda j, i: (i, j)),
        out_shape=jax.ShapeDtypeStruct((m, n), BF16),
        compiler_params=_cparams(("parallel", "arbitrary")),
        name="merge",
    )(ya, yb, wa, wb, gates, gates)


def _mm_res_kernel(x_ref, w_ref, r_ref, o_ref):
    o_ref[...] = r_ref[...] + jnp.dot(x_ref[...], w_ref[...], preferred_element_type=F32)


def _mm_res(x, w, res, *, tm, tn):
    m, k = x.shape
    n = w.shape[1]
    return pl.pallas_call(
        _mm_res_kernel,
        grid=(n // tn, m // tm),
        in_specs=[pl.BlockSpec((tm, k), lambda j, i: (i, 0)),
                  pl.BlockSpec((k, tn), lambda j, i: (0, j)),
                  pl.BlockSpec((tm, tn), lambda j, i: (i, j))],
        out_specs=pl.BlockSpec((tm, tn), lambda j, i: (i, j)),
        out_shape=jax.ShapeDtypeStruct((m, n), F32),
        compiler_params=_cparams(("parallel", "arbitrary")),
        name="out_proj",
    )(x, w, res)


def _page_block_means(pt_ref, ck_ref, km_ref, buf, sem, *, steps_per_seq, seq0):
    pps, page = buf.shape[1], buf.shape[2]
    ppb = MOBA_BLOCK // page
    step = pl.program_id(0) * pl.num_programs(1) + pl.program_id(1)
    nsteps = pl.num_programs(0) * pl.num_programs(1)

    def copies(step_, slot_):
        b_ = seq0 + step_ // steps_per_seq
        j_ = step_ % steps_per_seq
        return [pltpu.make_async_copy(ck_ref.at[pt_ref[b_, j_ * pps + i]], buf.at[slot_, i],
                                      sem.at[slot_]) for i in range(pps)]

    slot = step % 2

    @pl.when(step == 0)
    def _():
        for cp in copies(step, slot):
            cp.start()

    @pl.when(step + 1 < nsteps)
    def _():
        for cp in copies(step + 1, 1 - slot):
            cp.start()

    for cp in copies(step, slot):
        cp.wait()

    for i in range(pps // ppb):
        tot = functools.reduce(jnp.add, [jnp.sum(buf[slot, i * ppb + p], axis=0) for p in range(ppb)])
        km_ref[0, i] = tot * (1.0 / MOBA_BLOCK)


def _moba_prompt_kernel(pt_ref, q_ref, k_ref, v_ref, ck_ref, o_ref, km_ref, buf, sem, *,
                        seq, steps_per_seq, seq0):
    _page_block_means(pt_ref, ck_ref, km_ref, buf, sem, steps_per_seq=steps_per_seq, seq0=seq0)
    blk = MOBA_BLOCK
    nb = seq // blk
    dh = k_ref.shape[2]
    k32 = k_ref[0]
    kb = k32.astype(BF16)
    vt = jnp.concatenate([v_ref[0].T.astype(BF16), jnp.ones((BF16_ROWS, seq), BF16)], axis=0)
    kmean = jnp.concatenate(
        [jnp.mean(k32[n * blk:(n + 1) * blk], axis=0, keepdims=True) for n in range(nb)], axis=0)
    if nb % SUBLANES:
        kmean = jnp.concatenate(
            [kmean, jnp.zeros((SUBLANES - nb % SUBLANES, kmean.shape[1]), F32)], axis=0)
    km_hi = kmean.astype(BF16)
    km_lo = (kmean - km_hi.astype(F32)).astype(BF16)
    key_i = lax.broadcasted_iota(I32, (blk, blk), 0)
    qry_i = lax.broadcasted_iota(I32, (blk, blk), 1)
    causal = key_i <= qry_i

    for c in range(nb):
        q = q_ref[0, c * blk:(c + 1) * blk, :]
        blocks = []
        if c > 0:
            sel = None
            if c > MOBA_TOP_K:
                g = _nt(km_hi, q) + _nt(km_lo, q)
                rows = [g[n:n + 1, :] for n in range(c)]
                sel = []
                for n in range(c):
                    rank = jnp.zeros((1, blk), I32)
                    for m in range(c):
                        if m < n:
                            rank += (rows[m] >= rows[n]).astype(I32)
                        elif m > n:
                            rank += (rows[m] > rows[n]).astype(I32)
                    sel.append(rank < MOBA_TOP_K)
            for n in range(c):
                s = _nt(kb[n * blk:(n + 1) * blk], q)
                if sel is not None:
                    s = jnp.where(sel[n], s, NEG_INF)
                blocks.append(s)
        s_cur = _nt(kb[c * blk:(c + 1) * blk], q)
        blocks.append(jnp.where(causal, s_cur, NEG_INF))
        mx = functools.reduce(jnp.maximum, [jnp.max(b, axis=0, keepdims=True) for b in blocks])
        p_all = jnp.concatenate([jnp.exp2(b - mx).astype(BF16) for b in blocks], axis=0)
        ot = jnp.dot(vt[:, :(c + 1) * blk], p_all, preferred_element_type=F32)
        ot = ot[:dh] / ot[dh:dh + 1]
        o_ref[0, c * blk:(c + 1) * blk, :] = ot.T.astype(o_ref.dtype)


def _moba_prompt(q_src, q_col0, k, v, paged, *, batch, seq, heads):
    dh = HEAD_DIM_A
    assert seq % MOBA_BLOCK == 0
    plan = _PagedMeans(*paged, nsteps=batch * heads, grid_cols=heads)
    return pl.pallas_call(
        functools.partial(_moba_prompt_kernel, seq=seq, steps_per_seq=plan.steps_per_seq,
                          seq0=plan.seq0),
        grid_spec=pltpu.PrefetchScalarGridSpec(
            num_scalar_prefetch=1, grid=(batch, heads),
            in_specs=[pl.BlockSpec((1, seq, dh), lambda b, h, pt: (b, 0, q_col0 + h)),
                      pl.BlockSpec((1, seq, dh), lambda b, h, pt: (b, 0, h)),
                      pl.BlockSpec((1, seq, dh), lambda b, h, pt: (b, 0, h)),
                      pl.BlockSpec(memory_space=pl.ANY)],
            out_specs=(pl.BlockSpec((1, seq, dh), lambda b, h, pt: (b, 0, h)), plan.out_spec),
            scratch_shapes=plan.scratch),
        out_shape=(jax.ShapeDtypeStruct((batch, seq, heads * dh), BF16), plan.out_shape),
        compiler_params=_cparams(("arbitrary", "arbitrary")),
        name="moba_prompt",
    )(plan.page_table, q_src, k, v, plan.cache_k)


def _mlstm_kernel(q_ref, k_ref, v_ref, og_ref, gr_ref, gb_ref, nw_ref, c0_ref, n0_ref, m0_ref,
                  y_ref, c_ref, n_ref, m_ref, *, chunk, nchunks, dk):
    L = chunk
    r_i = lax.broadcasted_iota(I32, (L, L), 0)
    c_i = lax.broadcasted_iota(I32, (L, L), 1)
    causal = c_i <= r_i
    triu = (r_i <= c_i).astype(BF16)
    row_id = lax.broadcasted_iota(I32, (SUBLANES, L), 0)
    qscale = dk ** -0.5

    c_ref[0, 0] = c0_ref[0, 0]
    n_st = n0_ref[0, 0, 0:1, :]
    m_st = m0_ref[0, 0, 0:1, 0:1]
    nw = nw_ref[0]
    bias = gb_ref[0][:, 0:1]

    for t in range(nchunks):
        rows = slice(t * L, (t + 1) * L)
        g = gr_ref[:, rows] + bias
        lf = jnp.minimum(g, 0.0) - jnp.log1p(jnp.exp(-jnp.abs(g)))
        g = jnp.where(row_id == 1, lf, g)
        gh, gm, gl = _split3(g)
        cum = (jnp.dot(gh, triu, preferred_element_type=F32)
               + jnp.dot(gm, triu, preferred_element_type=F32)
               + jnp.dot(gl, triu, preferred_element_type=F32))
        both = jnp.concatenate(
            [g, cum, jnp.zeros((LANES - 2 * SUBLANES, L), F32)], axis=0)
        cols = both.T
        i_col = cols[:, 0:1]
        b_col = cols[:, SUBLANES + 1:SUBLANES + 2]
        i_row = g[0:1, :]
        b_row = cum[1:2, :]

        d = jnp.where(causal, b_col - b_row + i_row, NEG_INF)
        inter = b_col + m_st
        m_row = jnp.maximum(inter, jnp.max(d, axis=1, keepdims=True))
        w_intra = jnp.exp(d - m_row)
        w_inter = jnp.exp(inter - m_row)

        qc = q_ref[0, rows, :] * qscale
        kc = k_ref[0, rows, :]
        vc = v_ref[0, rows, :]
        c_st = c_ref[0, 0]
        s = _nt(qc, kc) * w_intra
        num = (jnp.dot(s.astype(BF16), vc, preferred_element_type=F32)
               + w_inter * jnp.dot(qc, c_st.astype(BF16), preferred_element_type=F32))
        qn = jnp.sum(qc.astype(F32) * n_st, axis=1, keepdims=True)
        den = jnp.sum(s, axis=1, keepdims=True) + w_inter * qn
        h = num / jnp.maximum(jnp.abs(den), jnp.exp(-m_row))
        hn = _rms(h, nw)
        y_ref[0, rows, :] = (og_ref[0, rows, :].astype(F32) * hn).astype(y_ref.dtype)

        m_new = m_row[L - 1:L, :]
        b_last = b_col[L - 1:L, :]
        decay = jnp.exp(b_last + m_st - m_new)
        w_col = jnp.exp(b_last - b_col + i_col - m_new)
        c_ref[0, 0] = decay * c_st + _tn(kc, (vc.astype(F32) * w_col).astype(BF16))
        n_st = decay * n_st + jnp.sum(kc.astype(F32) * w_col, axis=0, keepdims=True)
        m_st = m_new

    n_ref[0, 0] = jnp.broadcast_to(n_st, n_ref.shape[2:])
    m_ref[0, 0] = jnp.broadcast_to(m_st, m_ref.shape[2:])


def _mlstm(qkv_src, og_src, cols, gate_rows, gate_bias, norm_w, c0, n0, m0, *, batch, seq, chunk):
    nh = N_HEADS_B
    dk, dv = c0.shape[2], c0.shape[3]
    q0, k0, v0 = cols[0] // dk, cols[1] // dk, cols[2] // dv
    bh = lambda b, h: (b, h, 0, 0)
    return pl.pallas_call(
        functools.partial(_mlstm_kernel, chunk=chunk, nchunks=seq // chunk, dk=dk),
        grid=(batch, nh),
        in_specs=[pl.BlockSpec((1, seq, dk), lambda b, h: (b, 0, q0 + h)),
                  pl.BlockSpec((1, seq, dk), lambda b, h: (b, 0, k0 + h)),
                  pl.BlockSpec((1, seq, dv), lambda b, h: (b, 0, v0 + h)),
                  pl.BlockSpec((1, seq, dv), lambda b, h: (b, 0, h)),
                  pl.BlockSpec((SUBLANES, seq), lambda b, h: (h, b)),
                  pl.BlockSpec((1, SUBLANES, LANES), lambda b, h: (h, 0, 0)),
                  pl.BlockSpec((1, 1, dv), lambda b, h: (h, 0, 0)),
                  pl.BlockSpec((1, 1, dk, dv), bh),
                  pl.BlockSpec((1, 1, SUBLANES, dk), bh),
                  pl.BlockSpec((1, 1, SUBLANES, LANES), bh)],
        out_specs=(pl.BlockSpec((1, seq, dv), lambda b, h: (b, 0, h)),
                   pl.BlockSpec((1, 1, dk, dv), bh),
                   pl.BlockSpec((1, 1, SUBLANES, dk), bh),
                   pl.BlockSpec((1, 1, SUBLANES, LANES), bh)),
        out_shape=(jax.ShapeDtypeStruct((batch, seq, nh * dv), BF16),
                   jax.ShapeDtypeStruct((batch, nh, dk, dv), F32),
                   jax.ShapeDtypeStruct((batch, nh, SUBLANES, dk), F32),
                   jax.ShapeDtypeStruct((batch, nh, SUBLANES, LANES), F32)),
        compiler_params=_cparams(("parallel", "parallel")),
        name="mlstm",
    )(qkv_src, qkv_src, qkv_src, og_src, gate_rows, gate_bias, norm_w, c0, n0, m0)


def _topk_kernel(q_ref, km_ref, o_ref, *, heads):
    dh = HEAD_DIM_A
    nblk = km_ref.shape[1]
    lane = lax.broadcasted_iota(I32, (q_ref.shape[1], nblk), 1)
    lane_f = lane.astype(F32)
    out_lane = lax.broadcasted_iota(I32, (q_ref.shape[1], LANES), 1)
    for h in range(heads):
        q3 = _split3(q_ref[0, :, h * dh:(h + 1) * dh])
        k3 = _split3(km_ref[0, :, h, :])
        gate = functools.reduce(jnp.add, [_nt(q3[i], k3[j]) for i in range(3) for j in range(3 - i)])
        out = jnp.zeros((q_ref.shape[1], LANES), I32)
        for k in range(MOBA_TOP_K):
            mx = jnp.max(gate, axis=1, keepdims=True)
            idx = jnp.min(jnp.where(gate == mx, lane_f, float(nblk)), axis=1, keepdims=True).astype(I32)
            out = jnp.where(out_lane == k, idx, out)
            gate = jnp.where(lane == idx, -jnp.inf, gate)
        o_ref[0, h] = out


def _topk(q_pad, kmean, *, heads):
    batch, rows, width = q_pad.shape
    nblk = kmean.shape[1]
    return pl.pallas_call(
        functools.partial(_topk_kernel, heads=heads),
        grid=(batch,),
        in_specs=[pl.BlockSpec((1, rows, width), lambda b: (b, 0, 0)),
                  pl.BlockSpec((1, nblk, heads, HEAD_DIM_A), lambda b: (b, 0, 0, 0))],
        out_specs=pl.BlockSpec((1, heads, rows, LANES), lambda b: (b, 0, 0, 0)),
        out_shape=jax.ShapeDtypeStruct((batch, heads, rows, LANES), I32),
        compiler_params=_cparams(("parallel",)),
        name="moba_topk",
    )(q_pad, kmean)


def _moba_sample_kernel(sel_ref, pt_ref, q_ref, kn_ref, vn_ref, ck_ref, cv_ref, o_ref,
                        kbuf, vbuf, sem, *, heads, tokens):
    hps = SAMPLE_HEADS_PER_STEP
    groups = heads // hps
    step = pl.program_id(0) * groups + pl.program_id(1)
    nsteps = pl.num_programs(0) * groups
    page, dh = kbuf.shape[2], kbuf.shape[3]
    ppb = MOBA_BLOCK // page
    nsel = tokens * MOBA_TOP_K
    per_head = nsel * ppb

    def copies(step_, slot_):
        b_ = step_ // groups
        out = []
        for hh in range(hps):
            h_ = (step_ % groups) * hps + hh
            for j in range(nsel):
                blk = sel_ref[(b_ * heads + h_) * nsel + j]
                for p in range(ppb):
                    phys = pt_ref[b_, blk * ppb + p]
                    dst = hh * per_head + j * ppb + p
                    out.append(pltpu.make_async_copy(
                        ck_ref.at[phys, :, h_, :], kbuf.at[slot_, dst], sem.at[0, slot_]))
                    out.append(pltpu.make_async_copy(
                        cv_ref.at[phys, :, h_, :], vbuf.at[slot_, dst], sem.at[1, slot_]))
        return out

    slot = step % 2

    @pl.when(step == 0)
    def _():
        for cp in copies(step, slot):
            cp.start()

    @pl.when(step + 1 < nsteps)
    def _():
        for cp in copies(step + 1, 1 - slot):
            cp.start()

    for cp in copies(step, slot):
        cp.wait()

    npairs = hps * tokens
    per_tok = MOBA_TOP_K * ppb
    q = q_ref[0, 0]
    qb = q.astype(BF16)
    rows_n = q.shape[0]
    row_id = lax.broadcasted_iota(I32, (rows_n, 1), 0)
    s_sel = jnp.zeros((rows_n, per_tok * page), F32)
    for p in range(npairs):
        kp = kbuf[slot, p * per_tok:(p + 1) * per_tok].reshape(per_tok * page, dh).astype(BF16)
        s_sel = jnp.where(row_id == p, _nt(qb, kp), s_sel)
    pr = lax.broadcasted_iota(I32, (rows_n, rows_n), 0)
    pc = lax.broadcasted_iota(I32, (rows_n, rows_n), 1)
    ok = functools.reduce(jnp.logical_or, [
        (pr >= hh * tokens) & (pr < (hh + 1) * tokens) & (pc >= hh * tokens) & (pc <= pr)
        for hh in range(hps)])
    s_new = jnp.where(ok, _nt(qb, kn_ref[0, 0].astype(BF16)), NEG_INF)
    mx = jnp.maximum(jnp.max(s_sel, axis=1, keepdims=True), jnp.max(s_new, axis=1, keepdims=True))
    p_sel = jnp.exp2(s_sel - mx)
    p_new = jnp.exp2(s_new - mx)
    den = jnp.sum(p_sel, axis=1, keepdims=True) + jnp.sum(p_new, axis=1, keepdims=True)
    pb = p_sel.astype(BF16)
    o = jnp.zeros((rows_n, dh), F32)
    for p in range(npairs):
        vp = vbuf[slot, p * per_tok:(p + 1) * per_tok].reshape(per_tok * page, dh).astype(BF16)
        o = jnp.where(row_id == p, jnp.dot(pb, vp, preferred_element_type=F32), o)
    vn = vn_ref[0, 0]
    for c in range(npairs):
        o = o + p_new[:, c:c + 1] * vn[c:c + 1, :]
    o_ref[0, 0] = o / den


def _pair_rows(a, rows):
    b, t, width = a.shape
    hps, dh = SAMPLE_HEADS_PER_STEP, HEAD_DIM_A
    a = a.reshape(b, t, width // (hps * dh), hps, dh).transpose(0, 2, 3, 1, 4)
    a = a.reshape(b, width // (hps * dh), hps * t, dh)
    return jnp.pad(a, ((0, 0), (0, 0), (0, rows - hps * t), (0, 0)))


def _moba_sample(sel, page_table, q, k_new, v_new, cache_k, cache_v, *, heads):
    batch, tokens, width = q.shape
    dh = HEAD_DIM_A
    n_pool, page = cache_k.shape[0], cache_k.shape[1]
    ppb = MOBA_BLOCK // page
    hps = SAMPLE_HEADS_PER_STEP
    assert heads % hps == 0
    groups = heads // hps
    rows = -(-hps * tokens // BF16_ROWS) * BF16_ROWS
    nbuf = hps * tokens * MOBA_TOP_K * ppb
    blk = pl.BlockSpec((1, 1, rows, dh), lambda b, g, s, p: (b, g, 0, 0))
    o = pl.pallas_call(
        functools.partial(_moba_sample_kernel, heads=heads, tokens=tokens),
        grid_spec=pltpu.PrefetchScalarGridSpec(
            num_scalar_prefetch=2, grid=(batch, groups),
            in_specs=[blk, blk, blk,
                      pl.BlockSpec(memory_space=pl.ANY), pl.BlockSpec(memory_space=pl.ANY)],
            out_specs=blk,
            scratch_shapes=[pltpu.VMEM((2, nbuf, page, dh), F32),
                            pltpu.VMEM((2, nbuf, page, dh), F32),
                            pltpu.SemaphoreType.DMA((2, 2))]),
        out_shape=jax.ShapeDtypeStruct((batch, groups, rows, dh), F32),
        compiler_params=_cparams(("arbitrary", "arbitrary")),
        name="moba_sample",
    )(sel, page_table, _pair_rows(q, rows), _pair_rows(k_new, rows), _pair_rows(v_new, rows),
      cache_k, cache_v)
    o = o[:, :, :hps * tokens].reshape(batch, groups, hps, tokens, dh)
    return o.transpose(0, 3, 1, 2, 4).reshape(batch, tokens, width)


def _cast_pad_kernel(x_ref, o_ref, *, valid_steps):
    rows, cols = x_ref.shape
    x = x_ref[...].astype(o_ref.dtype)
    if valid_steps is not None:
        x = jnp.where(pl.program_id(0) < valid_steps, x, jnp.zeros_like(x))
    o_ref[:, :cols] = x
    if o_ref.shape[1] > cols:
        o_ref[:, cols:] = jnp.zeros((rows, o_ref.shape[1] - cols), o_ref.dtype)


def _cast_pad(w, *, row_mult=1, col_mult=1, tr):
    r, c = w.shape
    rows_to = -(-r // row_mult) * row_mult
    cols_to = -(-c // col_mult) * col_mult
    assert r % tr == 0 and rows_to % tr == 0 and c % LANES == 0
    valid = r // tr
    return pl.pallas_call(
        functools.partial(_cast_pad_kernel, valid_steps=valid if rows_to > r else None),
        grid=(rows_to // tr,),
        in_specs=[pl.BlockSpec((tr, c), lambda i: (jnp.minimum(i, valid - 1), 0))],
        out_specs=pl.BlockSpec((tr, cols_to), lambda i: (i, 0)),
        out_shape=jax.ShapeDtypeStruct((rows_to, cols_to), BF16),
        compiler_params=_cparams(("parallel",)),
        name="cast_pad",
    )(w)


def _regroup_kernel(x_ref, *o_refs, groups):
    for o_ref, parts in zip(o_refs, groups):
        at = 0
        for lo, hi in parts:
            o_ref[:, at:at + hi - lo] = x_ref[:, lo:hi].astype(o_ref.dtype)
            at += hi - lo


def _regroup_cols(w, groups, *, tr):
    r, c = w.shape
    widths = [sum(hi - lo for lo, hi in parts) for parts in groups]
    return pl.pallas_call(
        functools.partial(_regroup_kernel, groups=groups),
        grid=(r // tr,),
        in_specs=[pl.BlockSpec((tr, c), lambda i: (i, 0))],
        out_specs=tuple(pl.BlockSpec((tr, n), lambda i: (i, 0)) for n in widths),
        out_shape=tuple(jax.ShapeDtypeStruct((r, n), BF16) for n in widths),
        compiler_params=_cparams(("parallel",)),
        name="regroup_w_in",
    )(w)


def _prep_weights(ffn1_w_gate, ffn1_w_up, ffn1_w_down, w_in, w_branch_a, w_branch_b, w_out,
                  ffn2_w_gate, ffn2_w_up, ffn2_w_down, *, d_a, d_qk, d_b):
    bf = lambda w: w.astype(BF16)
    o = [0, d_a, 2 * d_a, 3 * d_a, 3 * d_a + d_qk, 3 * d_a + 2 * d_qk, 3 * d_a + 2 * d_qk + d_b,
         3 * d_a + 2 * d_qk + 2 * d_b]
    n_if = 2 * N_HEADS_B
    cols = lambda w: _cast_pad(w, col_mult=FF_TILE, tr=256)
    rows = lambda w: _cast_pad(w, row_mult=FF_TILE, tr=LANES)
    sq = lambda w: _cast_pad(w, tr=512)
    w_k, w_v, w_qkv, w_gates = _regroup_cols(
        w_in, (((o[1], o[2]),), ((o[2], o[3]),),
               ((o[0], o[1]), (o[3], o[6])),
               ((o[6], o[7]), (o[7] + n_if, w_in.shape[1]))),
        tr=LANES)
    return dict(
        f1=(cols(ffn1_w_gate), cols(ffn1_w_up), rows(ffn1_w_down)),
        f2=(cols(ffn2_w_gate), cols(ffn2_w_up), rows(ffn2_w_down)),
        w_k=w_k, w_v=w_v, w_qkv=w_qkv, w_gates=w_gates,
        w_if_t=bf(jnp.pad(w_in[:, o[7]:o[7] + n_if].T.reshape(2, N_HEADS_B, -1).transpose(1, 0, 2),
                          ((0, 0), (0, SUBLANES - 2), (0, 0))).reshape(N_HEADS_B * SUBLANES, -1)),
        w_a=sq(w_branch_a), w_b=sq(w_branch_b), w_o=sq(w_out))


def _pad_gate_rows(rows, batch, seq, pad_to):
    r = rows.shape[0]
    fill = jnp.zeros((SUBLANES,), F32).at[0].set(NEG_INF).at[1].set(-NEG_INF)
    fill = jnp.broadcast_to(jnp.tile(fill, r // SUBLANES)[:, None, None], (r, batch, pad_to - seq))
    return jnp.concatenate([rows.reshape(r, batch, seq), fill], axis=2).reshape(r, batch * pad_to)


def _pad_seq(a, rows):
    return jnp.pad(a, ((0, 0), (0, rows - a.shape[1]), (0, 0)))


def _layer(x, w, norms, b_if, attend, mlstm_in, *, batch, seq, tm, pool=None):
    ffn1_norm, mix_norm, mlstm_norm, ffn2_norm, final_norm = norms
    d = x.shape[1]
    nh = N_HEADS_B
    tn = 1024
    tn_wide = 2048
    d_a = w["w_k"].shape[1]
    x1, xn = _ffn(x, ffn1_norm, *w["f1"], mix_norm, tm=tm, final=False)
    k_a, v_a, if_rows = _mm2(xn, w["w_k"], w["w_v"], w["w_if_t"], tm=tm, tn=tn)
    qkv_args = dict(tm=tm, tn=tn_wide, out_dtype=BF16, lead_cols=d_a, lead_scale=QK_LOG2_SCALE,
                    name="proj_qkv")
    gate_args = dict(tm=tm, tn=tn_wide, out_dtype=BF16, act="sigmoid", name="proj_gates")
    if pool is None:
        qkv = _mm(xn, w["w_qkv"], **qkv_args)
        gates = _mm(xn, w["w_gates"], **gate_args)
        y_a = attend(qkv, k_a, v_a)
        kmean = None
    else:
        cache_k, page_table = pool
        n_seqs, n_pages = page_table.shape
        proj_steps = (w["w_qkv"].shape[1] // tn_wide) * (x.shape[0] // tm)
        assert (proj_steps * PROJ_PAGES_PER_STEP) % n_pages == 0
        s1 = proj_steps * PROJ_PAGES_PER_STEP // n_pages
        assert 2 * s1 < n_seqs
        qkv, km1 = _mm(xn, w["w_qkv"], paged=(cache_k, page_table, 0, s1), **qkv_args)
        gates, km2 = _mm(xn, w["w_gates"], paged=(cache_k, page_table, s1, s1), **gate_args)
        y_a, km3 = attend(qkv, k_a, v_a, (cache_k, page_table, 2 * s1, n_seqs - 2 * s1))
        kmean = jnp.concatenate([km1, km2, km3], axis=0)

    c0, n0, m0, chunk = mlstm_in
    dk, dv = c0.shape[2], c0.shape[3]
    cols = (d_a, d_a + nh * dk, d_a + 2 * nh * dk)
    gb = jnp.zeros((nh, SUBLANES, LANES), F32)
    gb = gb.at[:, 0, :].set(b_if[:nh, None]).at[:, 1, :].set(b_if[nh:, None])
    n0p = jnp.broadcast_to(n0[:, :, None, :], (batch, nh, SUBLANES, dk))
    m0p = jnp.broadcast_to(m0[:, :, None, None], (batch, nh, SUBLANES, LANES))
    if chunk > seq:
        qkv_m = _pad_seq(qkv.reshape(batch, seq, -1), chunk)
        og_m = _pad_seq(gates.reshape(batch, seq, -1), chunk)
        rows = _pad_gate_rows(if_rows, batch, seq, chunk)
        sp = chunk
    else:
        qkv_m, og_m = qkv.reshape(batch, seq, -1), gates.reshape(batch, seq, -1)
        rows = if_rows
        sp = seq
    y_b, c, n, m = _mlstm(qkv_m, og_m, cols, rows, gb, mlstm_norm.reshape(nh, 1, dv), c0, n0p, m0p,
                          batch=batch, seq=sp, chunk=chunk)
    y_b = y_b[:, :seq].reshape(batch * seq, nh * dv)

    mixed = _merge(y_a, y_b, w["w_a"], w["w_b"], gates, tm=tm, tn=tn)
    x2 = _mm_res(mixed, w["w_o"], x1, tm=tm, tn=tn)
    y = _ffn(x2, ffn2_norm, *w["f2"], final_norm, tm=tm, final=True)
    return y, k_a, v_a, (c, n[:, :, 0, :], m[:, :, 0, 0]), kmean


def kernel(x_prompt, x_sample, cache_k, cache_v, state_c, state_n, state_m, page_table, ffn1_norm, ffn1_w_gate, ffn1_w_up, ffn1_w_down, mix_norm, w_in, b_if, mlstm_norm, w_branch_a, w_branch_b, w_out, ffn2_norm, ffn2_w_gate, ffn2_w_up, ffn2_w_down, final_norm):
    depth = ffn1_norm.shape[0]
    assert depth == 1, "single-layer step"
    B, S, D = x_prompt.shape
    Bs, T, _ = x_sample.shape
    dh = HEAD_DIM_A
    nh_b = N_HEADS_B
    dk, dv = state_c.shape[3], state_c.shape[4]
    d_a = w_branch_a.shape[1]
    heads = d_a // dh
    n_pool, page = cache_k.shape[1], cache_k.shape[2]
    n_pages = page_table.shape[1]
    assert (n_pages * page) % MOBA_BLOCK == 0 and (n_pages * page) // MOBA_BLOCK >= MOBA_TOP_K
    l = 0

    w = _prep_weights(ffn1_w_gate[l], ffn1_w_up[l], ffn1_w_down[l], w_in[l], w_branch_a[l],
                      w_branch_b[l], w_out[l], ffn2_w_gate[l], ffn2_w_up[l], ffn2_w_down[l],
                      d_a=d_a, d_qk=nh_b * dk, d_b=nh_b * dv)
    norms = (ffn1_norm[l], mix_norm[l], mlstm_norm[l], ffn2_norm[l], final_norm)

    ck = cache_k.reshape(cache_k.shape[1:])
    cv = cache_v.reshape(cache_v.shape[1:])

    def attend_p(qkv, k_a, v_a, paged):
        o, km = _moba_prompt(qkv.reshape(B, S, -1), 0, k_a.reshape(B, S, d_a), v_a.reshape(B, S, d_a),
                             paged, batch=B, seq=S, heads=heads)
        return o.reshape(B * S, d_a), km

    st0 = (jnp.zeros((B, nh_b, dk, dv), F32), jnp.zeros((B, nh_b, dk), F32),
           jnp.zeros((B, nh_b), F32), MLSTM_CHUNK_PROMPT)
    yp, kp, vp, (cp, n_p, mp), kmean = _layer(x_prompt.reshape(B * S, D), w, norms, b_if[l], attend_p,
                                              st0, batch=B, seq=S, tm=512, pool=(ck, page_table))

    def attend_s(qkv, k_a, v_a):
        q = qkv[:, :d_a].astype(F32).reshape(Bs, T, d_a)
        top = _topk(_pad_seq(q, Q_PAD_ROWS), kmean, heads=heads)
        sel = top[:, :, :T, :MOBA_TOP_K].reshape(-1)
        o = _moba_sample(sel, page_table, q, k_a.reshape(Bs, T, d_a), v_a.reshape(Bs, T, d_a), ck, cv,
                         heads=heads)
        return o.reshape(Bs * T, d_a).astype(BF16)

    st_s = (state_c[l], state_n[l], state_m[l], MLSTM_CHUNK_SAMPLE)
    ys, k_s, v_s, (cs, n_s, ms), _ = _layer(x_sample.reshape(Bs * T, D), w, norms, b_if[l], attend_s,
                                            st_s, batch=Bs, seq=T, tm=Bs * T)

    return (yp.reshape(B, S, D), ys.reshape(Bs, T, D),
            kp.reshape(1, B, S, heads, dh), vp.reshape(1, B, S, heads, dh),
            cp[None], n_p[None], mp[None],
            k_s.reshape(1, Bs, T, heads, dh), v_s.reshape(1, Bs, T, heads, dh),
            cs[None], n_s[None], ms[None])
```

```python
import functools

import jax
import jax.numpy as jnp
from jax import lax
from jax.experimental import pallas as pl
from jax.experimental.pallas import tpu as pltpu

F32 = jnp.float32
BF16 = jnp.bfloat16
I32 = jnp.int32

HEAD_DIM_A = 128
MOBA_BLOCK = 256
MOBA_TOP_K = 3
N_HEADS_B = 4
EPS = 1e-6
NEG_INF = -1e30

LANES = 128
SUBLANES = 8
BF16_ROWS = 16
QK_LOG2_SCALE = HEAD_DIM_A ** -0.5 * 1.4426950408889634
VMEM_LIMIT_BYTES = 56 * 1024 * 1024
FF_TILE = 512
MLSTM_CHUNK_PROMPT = 256
MLSTM_CHUNK_SAMPLE = 128
SAMPLE_HEADS_PER_STEP = 4
PROJ_PAGES_PER_STEP = 8
Q_PAD_ROWS = BF16_ROWS


def _cparams(sem):
    return pltpu.CompilerParams(dimension_semantics=sem, vmem_limit_bytes=VMEM_LIMIT_BYTES)


def _rms(x, g):
    return x * lax.rsqrt(jnp.mean(x * x, axis=-1, keepdims=True) + EPS) * g


def _nt(a, b):
    return lax.dot_general(a, b, (((1,), (1,)), ((), ())), preferred_element_type=F32)


def _tn(a, b):
    return lax.dot_general(a, b, (((0,), (0,)), ((), ())), preferred_element_type=F32)


def _split3(x):
    hi = x.astype(BF16)
    r1 = x - hi.astype(F32)
    mid = r1.astype(BF16)
    lo = (r1 - mid.astype(F32)).astype(BF16)
    return hi, mid, lo


def _ffn_kernel(x_ref, g_ref, wg_ref, wu_ref, wd_ref, go_ref, *rest, final):
    if final:
        y_ref, xn_sc, acc_sc = rest
    else:
        y_ref, yn_ref, xn_sc, acc_sc = rest
    f = pl.program_id(1)

    @pl.when(f == 0)
    def _():
        xn_sc[...] = _rms(x_ref[...], g_ref[...]).astype(BF16)
        acc_sc[...] = jnp.zeros_like(acc_sc)

    xn = xn_sc[...]
    a = jnp.dot(xn, wg_ref[...], preferred_element_type=F32)
    u = jnp.dot(xn, wu_ref[...], preferred_element_type=F32)
    h = (a * jax.nn.sigmoid(a) * u).astype(BF16)
    acc_sc[...] += jnp.dot(h, wd_ref[...], preferred_element_type=F32)

    @pl.when(f == pl.num_programs(1) - 1)
    def _():
        y = x_ref[...] + 0.5 * acc_sc[...]
        if final:
            y_ref[...] = _rms(y, go_ref[...])
        else:
            y_ref[...] = y
            yn_ref[...] = _rms(y, go_ref[...]).astype(BF16)


def _ffn(x, g, wg, wu, wd, g_out, *, tm, final):
    m, d = x.shape
    ff = wg.shape[1]
    grid = (m // tm, ff // FF_TILE)
    row = pl.BlockSpec((tm, d), lambda i, f: (i, 0))
    vec = pl.BlockSpec((1, d), lambda i, f: (0, 0))
    in_specs = [row, vec,
                pl.BlockSpec((d, FF_TILE), lambda i, f: (0, f)),
                pl.BlockSpec((d, FF_TILE), lambda i, f: (0, f)),
                pl.BlockSpec((FF_TILE, d), lambda i, f: (f, 0)),
                vec]
    if final:
        out_shape = jax.ShapeDtypeStruct((m, d), F32)
        out_specs = row
    else:
        out_shape = (jax.ShapeDtypeStruct((m, d), F32), jax.ShapeDtypeStruct((m, d), BF16))
        out_specs = (row, row)
    return pl.pallas_call(
        functools.partial(_ffn_kernel, final=final),
        grid=grid, in_specs=in_specs, out_specs=out_specs, out_shape=out_shape,
        scratch_shapes=[pltpu.VMEM((tm, d), BF16), pltpu.VMEM((tm, d), F32)],
        compiler_params=_cparams(("parallel", "arbitrary")),
        name="ffn_final" if final else "ffn",
    )(x, g.reshape(1, d), wg, wu, wd, g_out.reshape(1, d))


class _PagedMeans:
    def __init__(self, cache_k, page_table, seq0, n_seqs, nsteps, grid_cols):
        n_pool, page, heads, dh = cache_k.shape
        n_pages = page_table.shape[1]
        ppb = MOBA_BLOCK // page
        assert (n_seqs * n_pages) % nsteps == 0
        pps = n_seqs * n_pages // nsteps
        assert pps % ppb == 0 and n_pages % pps == 0
        self.cache_k, self.page_table, self.seq0 = cache_k, page_table, seq0
        self.steps_per_seq = sps = n_pages // pps
        bps = pps // ppb
        self.out_spec = pl.BlockSpec(
            (1, bps, heads, dh),
            lambda a, b, pt: ((a * grid_cols + b) // sps, (a * grid_cols + b) % sps, 0, 0))
        self.out_shape = jax.ShapeDtypeStruct((n_seqs, n_pages // ppb, heads, dh), F32)
        self.scratch = [pltpu.VMEM((2, pps, page, heads, dh), F32), pltpu.SemaphoreType.DMA((2,))]


def _mm_kernel(*refs, act, lead_tiles, lead_scale, paged):
    if paged is None:
        x_ref, w_ref, o_ref = refs
    else:
        pt_ref, x_ref, w_ref, ck_ref, o_ref, km_ref, buf, sem = refs
        _page_block_means(pt_ref, ck_ref, km_ref, buf, sem, steps_per_seq=paged[0], seq0=paged[1])
    r = jnp.dot(x_ref[...], w_ref[...], preferred_element_type=F32)
    if act == "sigmoid":
        r = jax.nn.sigmoid(r)
    if lead_tiles:
        r = r * jnp.where(pl.program_id(0) < lead_tiles, lead_scale, 1.0)
    o_ref[...] = r.astype(o_ref.dtype)


def _mm(x, w, *, tm, tn, out_dtype, act=None, lead_cols=0, lead_scale=1.0, paged=None, name="proj"):
    m, k = x.shape
    n = w.shape[1]
    assert lead_cols % tn == 0
    grid = (n // tn, m // tm)
    in_specs = [pl.BlockSpec((tm, k), lambda j, i, *_: (i, 0)),
                pl.BlockSpec((k, tn), lambda j, i, *_: (0, j))]
    out_spec = pl.BlockSpec((tm, tn), lambda j, i, *_: (i, j))
    out_shape = jax.ShapeDtypeStruct((m, n), out_dtype)
    if paged is None:
        return pl.pallas_call(
            functools.partial(_mm_kernel, act=act, lead_tiles=lead_cols // tn, lead_scale=lead_scale,
                              paged=None),
            grid=grid, in_specs=in_specs, out_specs=out_spec, out_shape=out_shape,
            compiler_params=_cparams(("parallel", "arbitrary")),
            name=name,
        )(x, w)
    plan = _PagedMeans(*paged, nsteps=grid[0] * grid[1], grid_cols=grid[1])
    return pl.pallas_call(
        functools.partial(_mm_kernel, act=act, lead_tiles=lead_cols // tn, lead_scale=lead_scale,
                          paged=(plan.steps_per_seq, plan.seq0)),
        grid_spec=pltpu.PrefetchScalarGridSpec(
            num_scalar_prefetch=1, grid=grid,
            in_specs=in_specs + [pl.BlockSpec(memory_space=pl.ANY)],
            out_specs=(out_spec, plan.out_spec), scratch_shapes=plan.scratch),
        out_shape=(out_shape, plan.out_shape),
        compiler_params=_cparams(("arbitrary", "arbitrary")),
        name=name,
    )(plan.page_table, x, w, plan.cache_k)


def _mm2_kernel(x_ref, w1_ref, w2_ref, wt_ref, o1_ref, o2_ref, ot_ref):
    x = x_ref[...]
    o1_ref[...] = jnp.dot(x, w1_ref[...], preferred_element_type=F32)
    o2_ref[...] = jnp.dot(x, w2_ref[...], preferred_element_type=F32)
    ot_ref[0] = _nt(wt_ref[...], x)


def _mm2(x, w1, w2, wt, *, tm, tn):
    m, k = x.shape
    n = w1.shape[1]
    r = wt.shape[0]
    nj = n // tn
    wspec = pl.BlockSpec((k, tn), lambda j, i: (0, j))
    ospec = pl.BlockSpec((tm, tn), lambda j, i: (i, j))
    o1, o2, ot = pl.pallas_call(
        _mm2_kernel,
        grid=(nj, m // tm),
        in_specs=[pl.BlockSpec((tm, k), lambda j, i: (i, 0)), wspec, wspec,
                  pl.BlockSpec((r, k), lambda j, i: (0, 0))],
        out_specs=(ospec, ospec, pl.BlockSpec((1, r, tm), lambda j, i: (j, 0, i))),
        out_shape=(jax.ShapeDtypeStruct((m, n), F32), jax.ShapeDtypeStruct((m, n), F32),
                   jax.ShapeDtypeStruct((nj, r, m), F32)),
        compiler_params=_cparams(("parallel", "arbitrary")),
        name="proj_kv",
    )(x, w1, w2, wt)
    return o1, o2, ot[0]


def _merge_kernel(ya_ref, yb_ref, wa_ref, wb_ref, sga_ref, sgb_ref, o_ref):
    a = jnp.dot(ya_ref[...], wa_ref[...], preferred_element_type=F32)
    b = jnp.dot(yb_ref[...], wb_ref[...], preferred_element_type=F32)
    sga = jax.nn.sigmoid(sga_ref[...].astype(F32))
    sgb = jax.nn.sigmoid(sgb_ref[...].astype(F32))
    o_ref[...] = (sga * a + sgb * b).astype(o_ref.dtype)


def _merge(ya, yb, wa, wb, gates, *, tm, tn):
    m, k = ya.shape
    n = wa.shape[1]
    nj = n // tn
    xspec = pl.BlockSpec((tm, k), lambda j, i: (i, 0))
    wspec = pl.BlockSpec((k, tn), lambda j, i: (0, j))
    return pl.pallas_call(
        _merge_kernel,
        grid=(nj, m // tm),
        in_specs=[xspec, xspec, wspec, wspec,
                  pl.BlockSpec((tm, tn), lambda j, i: (i, nj + j)),
                  pl.BlockSpec((tm, tn), lambda j, i: (i, 2 * nj + j))],
        out_specs=pl.BlockSpec((tm, tn), lambda j, i: (i, j)),
        out_shape=jax.ShapeDtypeStruct((m, n), BF16),
        compiler_params=_cparams(("parallel", "arbitrary")),
        name="merge",
    )(ya, yb, wa, wb, gates, gates)


def _mm_res_kernel(x_ref, w_ref, r_ref, o_ref):
    o_ref[...] = r_ref[...] + jnp.dot(x_ref[...], w_ref[...], preferred_element_type=F32)


def _mm_res(x, w, res, *, tm, tn):
    m, k = x.shape
    n = w.shape[1]
    return pl.pallas_call(
        _mm_res_kernel,
        grid=(n // tn, m // tm),
        in_specs=[pl.BlockSpec((tm, k), lambda j, i: (i, 0)),
                  pl.BlockSpec((k, tn), lambda j, i: (0, j)),
                  pl.BlockSpec((tm, tn), lambda j, i: (i, j))],
        out_specs=pl.BlockSpec((tm, tn), lambda j, i: (i, j)),
        out_shape=jax.ShapeDtypeStruct((m, n), F32),
        compiler_params=_cparams(("parallel", "arbitrary")),
        name="out_proj",
    )(x, w, res)


def _page_block_means(pt_ref, ck_ref, km_ref, buf, sem, *, steps_per_seq, seq0):
    pps, page = buf.shape[1], buf.shape[2]
    ppb = MOBA_BLOCK // page
    step = pl.program_id(0) * pl.num_programs(1) + pl.program_id(1)
    nsteps = pl.num_programs(0) * pl.num_programs(1)

    def copies(step_, slot_):
        b_ = seq0 + step_ // steps_per_seq
        j_ = step_ % steps_per_seq
        return [pltpu.make_async_copy(ck_ref.at[pt_ref[b_, j_ * pps + i]], buf.at[slot_, i],
                                      sem.at[slot_]) for i in range(pps)]

    slot = step % 2

    @pl.when(step == 0)
    def _():
        for cp in copies(step, slot):
            cp.start()

    @pl.when(step + 1 < nsteps)
    def _():
        for cp in copies(step + 1, 1 - slot):
            cp.start()

    for cp in copies(step, slot):
        cp.wait()

    for i in range(pps // ppb):
        tot = functools.reduce(jnp.add, [jnp.sum(buf[slot, i * ppb + p], axis=0) for p in range(ppb)])
        km_ref[0, i] = tot * (1.0 / MOBA_BLOCK)


def _moba_prompt_kernel(pt_ref, q_ref, k_ref, v_ref, ck_ref, o_ref, km_ref, buf, sem, *,
                        seq, steps_per_seq, seq0):
    _page_block_means(pt_ref, ck_ref, km_ref, buf, sem, steps_per_seq=steps_per_seq, seq0=seq0)
    blk = MOBA_BLOCK
    nb = seq // blk
    dh = k_ref.shape[2]
    k32 = k_ref[0]
    kb = k32.astype(BF16)
    vt = jnp.concatenate([v_ref[0].T.astype(BF16), jnp.ones((BF16_ROWS, seq), BF16)], axis=0)
    kmean = jnp.concatenate(
        [jnp.mean(k32[n * blk:(n + 1) * blk], axis=0, keepdims=True) for n in range(nb)], axis=0)
    if nb % SUBLANES:
        kmean = jnp.concatenate(
            [kmean, jnp.zeros((SUBLANES - nb % SUBLANES, kmean.shape[1]), F32)], axis=0)
    km_hi = kmean.astype(BF16)
    km_lo = (kmean - km_hi.astype(F32)).astype(BF16)
    key_i = lax.broadcasted_iota(I32, (blk, blk), 0)
    qry_i = lax.broadcasted_iota(I32, (blk, blk), 1)
    causal = key_i <= qry_i

    for c in range(nb):
        q = q_ref[0, c * blk:(c + 1) * blk, :]
        blocks = []
        if c > 0:
            sel = None
            if c > MOBA_TOP_K:
                g = _nt(km_hi, q) + _nt(km_lo, q)
                rows = [g[n:n + 1, :] for n in range(c)]
                sel = []
                for n in range(c):
                    rank = jnp.zeros((1, blk), I32)
                    for m in range(c):
                        if m < n:
                            rank += (rows[m] >= rows[n]).astype(I32)
                        elif m > n:
                            rank += (rows[m] > rows[n]).astype(I32)
                    sel.append(rank < MOBA_TOP_K)
            for n in range(c):
                s = _nt(kb[n * blk:(n + 1) * blk], q)
                if sel is not None:
                    s = jnp.where(sel[n], s, NEG_INF)
                blocks.append(s)
        s_cur = _nt(kb[c * blk:(c + 1) * blk], q)
        blocks.append(jnp.where(causal, s_cur, NEG_INF))
        mx = functools.reduce(jnp.maximum, [jnp.max(b, axis=0, keepdims=True) for b in blocks])
        p_all = jnp.concatenate([jnp.exp2(b - mx).astype(BF16) for b in blocks], axis=0)
        ot = jnp.dot(vt[:, :(c + 1) * blk], p_all, preferred_element_type=F32)
        ot = ot[:dh] / ot[dh:dh + 1]
        o_ref[0, c * blk:(c + 1) * blk, :] = ot.T.astype(o_ref.dtype)


def _moba_prompt(q_src, q_col0, k, v, paged, *, batch, seq, heads):
    dh = HEAD_DIM_A
    assert seq % MOBA_BLOCK == 0
    plan = _PagedMeans(*paged, nsteps=batch * heads, grid_cols=heads)
    return pl.pallas_call(
        functools.partial(_moba_prompt_kernel, seq=seq, steps_per_seq=plan.steps_per_seq,
                          seq0=plan.seq0),
        grid_spec=pltpu.PrefetchScalarGridSpec(
            num_scalar_prefetch=1, grid=(batch, heads),
            in_specs=[pl.BlockSpec((1, seq, dh), lambda b, h, pt: (b, 0, q_col0 + h)),
                      pl.BlockSpec((1, seq, dh), lambda b, h, pt: (b, 0, h)),
                      pl.BlockSpec((1, seq, dh), lambda b, h, pt: (b, 0, h)),
                      pl.BlockSpec(memory_space=pl.ANY)],
            out_specs=(pl.BlockSpec((1, seq, dh), lambda b, h, pt: (b, 0, h)), plan.out_spec),
            scratch_shapes=plan.scratch),
        out_shape=(jax.ShapeDtypeStruct((batch, seq, heads * dh), BF16), plan.out_shape),
        compiler_params=_cparams(("arbitrary", "arbitrary")),
        name="moba_prompt",
    )(plan.page_table, q_src, k, v, plan.cache_k)


def _mlstm_kernel(q_ref, k_ref, v_ref, og_ref, gr_ref, gb_ref, nw_ref, c0_ref, n0_ref, m0_ref,
                  y_ref, c_ref, n_ref, m_ref, *, chunk, nchunks, dk):
    L = chunk
    r_i = lax.broadcasted_iota(I32, (L, L), 0)
    c_i = lax.broadcasted_iota(I32, (L, L), 1)
    causal = c_i <= r_i
    triu = (r_i <= c_i).astype(BF16)
    row_id = lax.broadcasted_iota(I32, (SUBLANES, L), 0)
    qscale = dk ** -0.5

    c_ref[0, 0] = c0_ref[0, 0]
    n_st = n0_ref[0, 0, 0:1, :]
    m_st = m0_ref[0, 0, 0:1, 0:1]
    nw = nw_ref[0]
    bias = gb_ref[0][:, 0:1]

    for t in range(nchunks):
        rows = slice(t * L, (t + 1) * L)
        g = gr_ref[:, rows] + bias
        lf = jnp.minimum(g, 0.0) - jnp.log1p(jnp.exp(-jnp.abs(g)))
        g = jnp.where(row_id == 1, lf, g)
        gh, gm, gl = _split3(g)
        cum = (jnp.dot(gh, triu, preferred_element_type=F32)
               + jnp.dot(gm, triu, preferred_element_type=F32)
               + jnp.dot(gl, triu, preferred_element_type=F32))
        both = jnp.concatenate(
            [g, cum, jnp.zeros((LANES - 2 * SUBLANES, L), F32)], axis=0)
        cols = both.T
        i_col = cols[:, 0:1]
        b_col = cols[:, SUBLANES + 1:SUBLANES + 2]
        i_row = g[0:1, :]
        b_row = cum[1:2, :]

        d = jnp.where(causal, b_col - b_row + i_row, NEG_INF)
        inter = b_col + m_st
        m_row = jnp.maximum(inter, jnp.max(d, axis=1, keepdims=True))
        w_intra = jnp.exp(d - m_row)
        w_inter = jnp.exp(inter - m_row)

        qc = q_ref[0, rows, :] * qscale
        kc = k_ref[0, rows, :]
        vc = v_ref[0, rows, :]
        c_st = c_ref[0, 0]
        s = _nt(qc, kc) * w_intra
        num = (jnp.dot(s.astype(BF16), vc, preferred_element_type=F32)
               + w_inter * jnp.dot(qc, c_st.astype(BF16), preferred_element_type=F32))
        qn = jnp.sum(qc.astype(F32) * n_st, axis=1, keepdims=True)
        den = jnp.sum(s, axis=1, keepdims=True) + w_inter * qn
        h = num / jnp.maximum(jnp.abs(den), jnp.exp(-m_row))
        hn = _rms(h, nw)
        og = jax.nn.sigmoid(og_ref[0, rows, :].astype(F32))
        y_ref[0, rows, :] = (og * hn).astype(y_ref.dtype)

        m_new = m_row[L - 1:L, :]
        b_last = b_col[L - 1:L, :]
        decay = jnp.exp(b_last + m_st - m_new)
        w_col = jnp.exp(b_last - b_col + i_col - m_new)
        c_ref[0, 0] = decay * c_st + _tn(kc, (vc.astype(F32) * w_col).astype(BF16))
        n_st = decay * n_st + jnp.sum(kc.astype(F32) * w_col, axis=0, keepdims=True)
        m_st = m_new

    n_ref[0, 0] = jnp.broadcast_to(n_st, n_ref.shape[2:])
    m_ref[0, 0] = jnp.broadcast_to(m_st, m_ref.shape[2:])


def _mlstm(qkv_src, og_src, cols, gate_rows, gate_bias, norm_w, c0, n0, m0, *, batch, seq, chunk):
    nh = N_HEADS_B
    dk, dv = c0.shape[2], c0.shape[3]
    q0, k0, v0 = cols[0] // dk, cols[1] // dk, cols[2] // dv
    bh = lambda b, h: (b, h, 0, 0)
    return pl.pallas_call(
        functools.partial(_mlstm_kernel, chunk=chunk, nchunks=seq // chunk, dk=dk),
        grid=(batch, nh),
        in_specs=[pl.BlockSpec((1, seq, dk), lambda b, h: (b, 0, q0 + h)),
                  pl.BlockSpec((1, seq, dk), lambda b, h: (b, 0, k0 + h)),
                  pl.BlockSpec((1, seq, dv), lambda b, h: (b, 0, v0 + h)),
                  pl.BlockSpec((1, seq, dv), lambda b, h: (b, 0, h)),
                  pl.BlockSpec((SUBLANES, seq), lambda b, h: (h, b)),
                  pl.BlockSpec((1, SUBLANES, LANES), lambda b, h: (h, 0, 0)),
                  pl.BlockSpec((1, 1, dv), lambda b, h: (h, 0, 0)),
                  pl.BlockSpec((1, 1, dk, dv), bh),
                  pl.BlockSpec((1, 1, SUBLANES, dk), bh),
                  pl.BlockSpec((1, 1, SUBLANES, LANES), bh)],
        out_specs=(pl.BlockSpec((1, seq, dv), lambda b, h: (b, 0, h)),
                   pl.BlockSpec((1, 1, dk, dv), bh),
                   pl.BlockSpec((1, 1, SUBLANES, dk), bh),
                   pl.BlockSpec((1, 1, SUBLANES, LANES), bh)),
        out_shape=(jax.ShapeDtypeStruct((batch, seq, nh * dv), BF16),
                   jax.ShapeDtypeStruct((batch, nh, dk, dv), F32),
                   jax.ShapeDtypeStruct((batch, nh, SUBLANES, dk), F32),
                   jax.ShapeDtypeStruct((batch, nh, SUBLANES, LANES), F32)),
        compiler_params=_cparams(("parallel", "parallel")),
        name="mlstm",
    )(qkv_src, qkv_src, qkv_src, og_src, gate_rows, gate_bias, norm_w, c0, n0, m0)


def _topk_kernel(q_ref, km_ref, o_ref, *, heads):
    dh = HEAD_DIM_A
    nblk = km_ref.shape[1]
    lane = lax.broadcasted_iota(I32, (q_ref.shape[1], nblk), 1)
    lane_f = lane.astype(F32)
    out_lane = lax.broadcasted_iota(I32, (q_ref.shape[1], LANES), 1)
    for h in range(heads):
        q3 = _split3(q_ref[0, :, h * dh:(h + 1) * dh])
        k3 = _split3(km_ref[0, :, h, :])
        gate = functools.reduce(jnp.add, [_nt(q3[i], k3[j]) for i in range(3) for j in range(3 - i)])
        out = jnp.zeros((q_ref.shape[1], LANES), I32)
        for k in range(MOBA_TOP_K):
            mx = jnp.max(gate, axis=1, keepdims=True)
            idx = jnp.min(jnp.where(gate == mx, lane_f, float(nblk)), axis=1, keepdims=True).astype(I32)
            out = jnp.where(out_lane == k, idx, out)
            gate = jnp.where(lane == idx, -jnp.inf, gate)
        o_ref[0, h] = out


def _topk(q_pad, kmean, *, heads):
    batch, rows, width = q_pad.shape
    nblk = kmean.shape[1]
    return pl.pallas_call(
        functools.partial(_topk_kernel, heads=heads),
        grid=(batch,),
        in_specs=[pl.BlockSpec((1, rows, width), lambda b: (b, 0, 0)),
                  pl.BlockSpec((1, nblk, heads, HEAD_DIM_A), lambda b: (b, 0, 0, 0))],
        out_specs=pl.BlockSpec((1, heads, rows, LANES), lambda b: (b, 0, 0, 0)),
        out_shape=jax.ShapeDtypeStruct((batch, heads, rows, LANES), I32),
        compiler_params=_cparams(("parallel",)),
        name="moba_topk",
    )(q_pad, kmean)


def _moba_sample_kernel(sel_ref, pt_ref, q_ref, kn_ref, vn_ref, ck_ref, cv_ref, o_ref,
                        kbuf, vbuf, sem, *, heads, tokens):
    hps = SAMPLE_HEADS_PER_STEP
    groups = heads // hps
    step = pl.program_id(0) * groups + pl.program_id(1)
    nsteps = pl.num_programs(0) * groups
    page, dh = kbuf.shape[2], kbuf.shape[3]
    ppb = MOBA_BLOCK // page
    nsel = tokens * MOBA_TOP_K
    per_head = nsel * ppb

    def copies(step_, slot_):
        b_ = step_ // groups
        out = []
        for hh in range(hps):
            h_ = (step_ % groups) * hps + hh
            for j in range(nsel):
                blk = sel_ref[(b_ * heads + h_) * nsel + j]
                for p in range(ppb):
                    phys = pt_ref[b_, blk * ppb + p]
                    dst = hh * per_head + j * ppb + p
                    out.append(pltpu.make_async_copy(
                        ck_ref.at[phys, :, h_, :], kbuf.at[slot_, dst], sem.at[0, slot_]))
                    out.append(pltpu.make_async_copy(
                        cv_ref.at[phys, :, h_, :], vbuf.at[slot_, dst], sem.at[1, slot_]))
        return out

    slot = step % 2

    @pl.when(step == 0)
    def _():
        for cp in copies(step, slot):
            cp.start()

    @pl.when(step + 1 < nsteps)
    def _():
        for cp in copies(step + 1, 1 - slot):
            cp.start()

    for cp in copies(step, slot):
        cp.wait()

    npairs = hps * tokens
    per_tok = MOBA_TOP_K * ppb
    q = q_ref[0, 0]
    qb = q.astype(BF16)
    rows_n = q.shape[0]
    row_id = lax.broadcasted_iota(I32, (rows_n, 1), 0)
    s_sel = jnp.zeros((rows_n, per_tok * page), F32)
    for p in range(npairs):
        kp = kbuf[slot, p * per_tok:(p + 1) * per_tok].reshape(per_tok * page, dh).astype(BF16)
        s_sel = jnp.where(row_id == p, _nt(qb, kp), s_sel)
    pr = lax.broadcasted_iota(I32, (rows_n, rows_n), 0)
    pc = lax.broadcasted_iota(I32, (rows_n, rows_n), 1)
    ok = functools.reduce(jnp.logical_or, [
        (pr >= hh * tokens) & (pr < (hh + 1) * tokens) & (pc >= hh * tokens) & (pc <= pr)
        for hh in range(hps)])
    s_new = jnp.where(ok, _nt(qb, kn_ref[0, 0].astype(BF16)), NEG_INF)
    mx = jnp.maximum(jnp.max(s_sel, axis=1, keepdims=True), jnp.max(s_new, axis=1, keepdims=True))
    p_sel = jnp.exp2(s_sel - mx)
    p_new = jnp.exp2(s_new - mx)
    den = jnp.sum(p_sel, axis=1, keepdims=True) + jnp.sum(p_new, axis=1, keepdims=True)
    pb = p_sel.astype(BF16)
    o = jnp.zeros((rows_n, dh), F32)
    for p in range(npairs):
        vp = vbuf[slot, p * per_tok:(p + 1) * per_tok].reshape(per_tok * page, dh).astype(BF16)
        o = jnp.where(row_id == p, jnp.dot(pb, vp, preferred_element_type=F32), o)
    vn = vn_ref[0, 0]
    for c in range(npairs):
        o = o + p_new[:, c:c + 1] * vn[c:c + 1, :]
    o_ref[0, 0] = o / den


def _pair_rows(a, rows):
    b, t, width = a.shape
    hps, dh = SAMPLE_HEADS_PER_STEP, HEAD_DIM_A
    a = a.reshape(b, t, width // (hps * dh), hps, dh).transpose(0, 2, 3, 1, 4)
    a = a.reshape(b, width // (hps * dh), hps * t, dh)
    return jnp.pad(a, ((0, 0), (0, 0), (0, rows - hps * t), (0, 0)))


def _moba_sample(sel, page_table, q, k_new, v_new, cache_k, cache_v, *, heads):
    batch, tokens, width = q.shape
    dh = HEAD_DIM_A
    n_pool, page = cache_k.shape[0], cache_k.shape[1]
    ppb = MOBA_BLOCK // page
    hps = SAMPLE_HEADS_PER_STEP
    assert heads % hps == 0
    groups = heads // hps
    rows = -(-hps * tokens // BF16_ROWS) * BF16_ROWS
    nbuf = hps * tokens * MOBA_TOP_K * ppb
    blk = pl.BlockSpec((1, 1, rows, dh), lambda b, g, s, p: (b, g, 0, 0))
    o = pl.pallas_call(
        functools.partial(_moba_sample_kernel, heads=heads, tokens=tokens),
        grid_spec=pltpu.PrefetchScalarGridSpec(
            num_scalar_prefetch=2, grid=(batch, groups),
            in_specs=[blk, blk, blk,
                      pl.BlockSpec(memory_space=pl.ANY), pl.BlockSpec(memory_space=pl.ANY)],
            out_specs=blk,
            scratch_shapes=[pltpu.VMEM((2, nbuf, page, dh), F32),
                            pltpu.VMEM((2, nbuf, page, dh), F32),
                            pltpu.SemaphoreType.DMA((2, 2))]),
        out_shape=jax.ShapeDtypeStruct((batch, groups, rows, dh), F32),
        compiler_params=_cparams(("arbitrary", "arbitrary")),
        name="moba_sample",
    )(sel, page_table, _pair_rows(q, rows), _pair_rows(k_new, rows), _pair_rows(v_new, rows),
      cache_k, cache_v)
    o = o[:, :, :hps * tokens].reshape(batch, groups, hps, tokens, dh)
    return o.transpose(0, 3, 1, 2, 4).reshape(batch, tokens, width)


def _cast_pad_kernel(x_ref, o_ref, *, valid_steps):
    rows, cols = x_ref.shape
    x = x_ref[...].astype(o_ref.dtype)
    if valid_steps is not None:
        x = jnp.where(pl.program_id(0) < valid_steps, x, jnp.zeros_like(x))
    o_ref[:, :cols] = x
    if o_ref.shape[1] > cols:
        o_ref[:, cols:] = jnp.zeros((rows, o_ref.shape[1] - cols), o_ref.dtype)


def _cast_pad(w, *, row_mult=1, col_mult=1, tr):
    r, c = w.shape
    rows_to = -(-r // row_mult) * row_mult
    cols_to = -(-c // col_mult) * col_mult
    assert r % tr == 0 and rows_to % tr == 0 and c % LANES == 0
    valid = r // tr
    return pl.pallas_call(
        functools.partial(_cast_pad_kernel, valid_steps=valid if rows_to > r else None),
        grid=(rows_to // tr,),
        in_specs=[pl.BlockSpec((tr, c), lambda i: (jnp.minimum(i, valid - 1), 0))],
        out_specs=pl.BlockSpec((tr, cols_to), lambda i: (i, 0)),
        out_shape=jax.ShapeDtypeStruct((rows_to, cols_to), BF16),
        compiler_params=_cparams(("parallel",)),
        name="cast_pad",
    )(w)


def _regroup_kernel(x_ref, *o_refs, groups):
    for o_ref, parts in zip(o_refs, groups):
        at = 0
        for lo, hi in parts:
            o_ref[:, at:at + hi - lo] = x_ref[:, lo:hi].astype(o_ref.dtype)
            at += hi - lo


def _regroup_cols(w, groups, *, tr):
    r, c = w.shape
    widths = [sum(hi - lo for lo, hi in parts) for parts in groups]
    return pl.pallas_call(
        functools.partial(_regroup_kernel, groups=groups),
        grid=(r // tr,),
        in_specs=[pl.BlockSpec((tr, c), lambda i: (i, 0))],
        out_specs=tuple(pl.BlockSpec((tr, n), lambda i: (i, 0)) for n in widths),
        out_shape=tuple(jax.ShapeDtypeStruct((r, n), BF16) for n in widths),
        compiler_params=_cparams(("parallel",)),
        name="regroup_w_in",
    )(w)


def _prep_weights(ffn1_w_gate, ffn1_w_up, ffn1_w_down, w_in, w_branch_a, w_branch_b, w_out,
                  ffn2_w_gate, ffn2_w_up, ffn2_w_down, *, d_a, d_qk, d_b):
    bf = lambda w: w.astype(BF16)
    o = [0, d_a, 2 * d_a, 3 * d_a, 3 * d_a + d_qk, 3 * d_a + 2 * d_qk, 3 * d_a + 2 * d_qk + d_b,
         3 * d_a + 2 * d_qk + 2 * d_b]
    n_if = 2 * N_HEADS_B
    cols = lambda w: _cast_pad(w, col_mult=FF_TILE, tr=256)
    rows = lambda w: _cast_pad(w, row_mult=FF_TILE, tr=LANES)
    sq = lambda w: _cast_pad(w, tr=512)
    w_k, w_v, w_qkv, w_gates = _regroup_cols(
        w_in, (((o[1], o[2]),), ((o[2], o[3]),),
               ((o[0], o[1]), (o[3], o[6])),
               ((o[6], o[7]), (o[7] + n_if, w_in.shape[1]))),
        tr=LANES)
    return dict(
        f1=(cols(ffn1_w_gate), cols(ffn1_w_up), rows(ffn1_w_down)),
        f2=(cols(ffn2_w_gate), cols(ffn2_w_up), rows(ffn2_w_down)),
        w_k=w_k, w_v=w_v, w_qkv=w_qkv, w_gates=w_gates,
        w_if_t=bf(jnp.pad(w_in[:, o[7]:o[7] + n_if].T.reshape(2, N_HEADS_B, -1).transpose(1, 0, 2),
                          ((0, 0), (0, SUBLANES - 2), (0, 0))).reshape(N_HEADS_B * SUBLANES, -1)),
        w_a=sq(w_branch_a), w_b=sq(w_branch_b), w_o=sq(w_out))


def _pad_gate_rows(rows, batch, seq, pad_to):
    r = rows.shape[0]
    fill = jnp.zeros((SUBLANES,), F32).at[0].set(NEG_INF).at[1].set(-NEG_INF)
    fill = jnp.broadcast_to(jnp.tile(fill, r // SUBLANES)[:, None, None], (r, batch, pad_to - seq))
    return jnp.concatenate([rows.reshape(r, batch, seq), fill], axis=2).reshape(r, batch * pad_to)


def _pad_seq(a, rows):
    return jnp.pad(a, ((0, 0), (0, rows - a.shape[1]), (0, 0)))


def _layer(x, w, norms, b_if, attend, mlstm_in, *, batch, seq, tm, pool=None):
    ffn1_norm, mix_norm, mlstm_norm, ffn2_norm, final_norm = norms
    d = x.shape[1]
    nh = N_HEADS_B
    tn = 1024
    tn_wide = 2048
    d_a = w["w_k"].shape[1]
    x1, xn = _ffn(x, ffn1_norm, *w["f1"], mix_norm, tm=tm, final=False)
    k_a, v_a, if_rows = _mm2(xn, w["w_k"], w["w_v"], w["w_if_t"], tm=tm, tn=tn)
    qkv_args = dict(tm=tm, tn=tn_wide, out_dtype=BF16, lead_cols=d_a, lead_scale=QK_LOG2_SCALE,
                    name="proj_qkv")
    gate_args = dict(tm=tm, tn=tn_wide, out_dtype=BF16, name="proj_gates")
    if pool is None:
        qkv = _mm(xn, w["w_qkv"], **qkv_args)
        gates = _mm(xn, w["w_gates"], **gate_args)
        y_a = attend(qkv, k_a, v_a)
        kmean = None
    else:
        cache_k, page_table = pool
        n_seqs, n_pages = page_table.shape
        proj_steps = (w["w_qkv"].shape[1] // tn_wide) * (x.shape[0] // tm)
        assert (proj_steps * PROJ_PAGES_PER_STEP) % n_pages == 0
        s1 = proj_steps * PROJ_PAGES_PER_STEP // n_pages
        assert 2 * s1 < n_seqs
        qkv, km1 = _mm(xn, w["w_qkv"], paged=(cache_k, page_table, 0, s1), **qkv_args)
        gates, km2 = _mm(xn, w["w_gates"], paged=(cache_k, page_table, s1, s1), **gate_args)
        y_a, km3 = attend(qkv, k_a, v_a, (cache_k, page_table, 2 * s1, n_seqs - 2 * s1))
        kmean = jnp.concatenate([km1, km2, km3], axis=0)

    c0, n0, m0, chunk = mlstm_in
    dk, dv = c0.shape[2], c0.shape[3]
    cols = (d_a, d_a + nh * dk, d_a + 2 * nh * dk)
    gb = jnp.zeros((nh, SUBLANES, LANES), F32)
    gb = gb.at[:, 0, :].set(b_if[:nh, None]).at[:, 1, :].set(b_if[nh:, None])
    n0p = jnp.broadcast_to(n0[:, :, None, :], (batch, nh, SUBLANES, dk))
    m0p = jnp.broadcast_to(m0[:, :, None, None], (batch, nh, SUBLANES, LANES))
    if chunk > seq:
        qkv_m = _pad_seq(qkv.reshape(batch, seq, -1), chunk)
        og_m = _pad_seq(gates.reshape(batch, seq, -1), chunk)
        rows = _pad_gate_rows(if_rows, batch, seq, chunk)
        sp = chunk
    else:
        qkv_m, og_m = qkv.reshape(batch, seq, -1), gates.reshape(batch, seq, -1)
        rows = if_rows
        sp = seq
    y_b, c, n, m = _mlstm(qkv_m, og_m, cols, rows, gb, mlstm_norm.reshape(nh, 1, dv), c0, n0p, m0p,
                          batch=batch, seq=sp, chunk=chunk)
    y_b = y_b[:, :seq].reshape(batch * seq, nh * dv)

    mixed = _merge(y_a, y_b, w["w_a"], w["w_b"], gates, tm=tm, tn=tn)
    x2 = _mm_res(mixed, w["w_o"], x1, tm=tm, tn=tn)
    y = _ffn(x2, ffn2_norm, *w["f2"], final_norm, tm=tm, final=True)
    return y, k_a, v_a, (c, n[:, :, 0, :], m[:, :, 0, 0]), kmean


def kernel(x_prompt, x_sample, cache_k, cache_v, state_c, state_n, state_m, page_table, ffn1_norm, ffn1_w_gate, ffn1_w_up, ffn1_w_down, mix_norm, w_in, b_if, mlstm_norm, w_branch_a, w_branch_b, w_out, ffn2_norm, ffn2_w_gate, ffn2_w_up, ffn2_w_down, final_norm):
    depth = ffn1_norm.shape[0]
    assert depth == 1, "single-layer step"
    B, S, D = x_prompt.shape
    Bs, T, _ = x_sample.shape
    dh = HEAD_DIM_A
    nh_b = N_HEADS_B
    dk, dv = state_c.shape[3], state_c.shape[4]
    d_a = w_branch_a.shape[1]
    heads = d_a // dh
    n_pool, page = cache_k.shape[1], cache_k.shape[2]
    n_pages = page_table.shape[1]
    assert (n_pages * page) % MOBA_BLOCK == 0 and (n_pages * page) // MOBA_BLOCK >= MOBA_TOP_K
    l = 0

    w = _prep_weights(ffn1_w_gate[l], ffn1_w_up[l], ffn1_w_down[l], w_in[l], w_branch_a[l],
                      w_branch_b[l], w_out[l], ffn2_w_gate[l], ffn2_w_up[l], ffn2_w_down[l],
                      d_a=d_a, d_qk=nh_b * dk, d_b=nh_b * dv)
    norms = (ffn1_norm[l], mix_norm[l], mlstm_norm[l], ffn2_norm[l], final_norm)

    ck = cache_k.reshape(cache_k.shape[1:])
    cv = cache_v.reshape(cache_v.shape[1:])

    def attend_p(qkv, k_a, v_a, paged):
        o, km = _moba_prompt(qkv.reshape(B, S, -1), 0, k_a.reshape(B, S, d_a), v_a.reshape(B, S, d_a),
                             paged, batch=B, seq=S, heads=heads)
        return o.reshape(B * S, d_a), km

    st0 = (jnp.zeros((B, nh_b, dk, dv), F32), jnp.zeros((B, nh_b, dk), F32),
           jnp.zeros((B, nh_b), F32), MLSTM_CHUNK_PROMPT)
    yp, kp, vp, (cp, n_p, mp), kmean = _layer(x_prompt.reshape(B * S, D), w, norms, b_if[l], attend_p,
                                              st0, batch=B, seq=S, tm=512, pool=(ck, page_table))

    def attend_s(qkv, k_a, v_a):
        q = qkv[:, :d_a].astype(F32).reshape(Bs, T, d_a)
        top = _topk(_pad_seq(q, Q_PAD_ROWS), kmean, heads=heads)
        sel = top[:, :, :T, :MOBA_TOP_K].reshape(-1)
        o = _moba_sample(sel, page_table, q, k_a.reshape(Bs, T, d_a), v_a.reshape(Bs, T, d_a), ck, cv,
                         heads=heads)
        return o.reshape(Bs * T, d_a).astype(BF16)

    st_s = (state_c[l], state_n[l], state_m[l], MLSTM_CHUNK_SAMPLE)
    ys, k_s, v_s, (cs, n_s, ms), _ = _layer(x_sample.reshape(Bs * T, D), w, norms, b_if[l], attend_s,
                                            st_s, batch=Bs, seq=T, tm=Bs * T)

    return (yp.reshape(B, S, D), ys.reshape(Bs, T, D),
            kp.reshape(1, B, S, heads, dh), vp.reshape(1, B, S, heads, dh),
            cp[None], n_p[None], mp[None],
            k_s.reshape(1, Bs, T, heads, dh), v_s.reshape(1, Bs, T, heads, dh),
            cs[None], n_s[None], ms[None])
```
